```python
import math
import jax
import jax.numpy as jnp
from jax import lax
import numpy as np

D_MODEL = 2048
BATCH = 2
SEQ = 4096
DEPTH = 4

GRID_W = 64
CTX_LEN = 256
DN_WIDTH = D_MODEL // 2
DN_HEAD_DIM = 128
DN_HEADS = DN_WIDTH // DN_HEAD_DIM
RW_WIDTH = D_MODEL - DN_WIDTH
RW_HEAD_DIM = 64
RW_HEADS = RW_WIDTH // RW_HEAD_DIM
CONV_K = 7
CHUNK = 64
W_LORA = 64
A_LORA = 64
G_LORA = 160
FFN_HIDDEN = -(-8 * D_MODEL // (3 * 256)) * 256
P_DN = 4 * DN_WIDTH + 4 * DN_HEADS
P_RW = 3 * RW_WIDTH + 2 * W_LORA + 2 * A_LORA + G_LORA
P_IN = P_DN + P_RW
NORM_EPS = 1e-6
RW_LN_EPS = 64e-5

kernel_name = 'hybrid_deltanet_rwkv7_flow_block'


def rms_norm(x, gain):
    xf = x.astype(jnp.float32)
    y = xf * lax.rsqrt(jnp.mean(xf * xf, axis=-1, keepdims=True) + NORM_EPS)
    return y.astype(x.dtype) * gain


def modulate(x, shift, scale):
    return x * (1.0 + scale) + shift


def l2_normalize(x):
    return x * lax.rsqrt(jnp.sum(x * x, axis=-1, keepdims=True) + NORM_EPS)


def to_col_major(a, rows):
    b, t, ch = a.shape
    return a.reshape(b, rows, GRID_W, ch).transpose(0, 2, 1, 3).reshape(b, t, ch)


def to_row_major(a, rows):
    b, t, ch = a.shape
    return a.reshape(b, GRID_W, rows, ch).transpose(0, 2, 1, 3).reshape(b, t, ch)


def centred_conv(a, w):
    ch = a.shape[-1]
    return lax.conv_general_dilated(
        a, w.astype(a.dtype)[:, None, :], window_strides=(1,),
        padding=[(CONV_K // 2, CONV_K // 2)],
        dimension_numbers=('NWC', 'WIO', 'NWC'), feature_group_count=ch)


def dn_heads(a):
    b, t, _ = a.shape
    return a.reshape(b, t, DN_HEADS, DN_HEAD_DIM).transpose(0, 2, 1, 3)


def gated_delta_rule_chunked(q, k, v, g, beta, state):
    b, nh, t_len, _ = q.shape
    dv = v.shape[-1]
    n = t_len // CHUNK

    def chunks(a):
        return a.reshape(b, nh, n, CHUNK, *a.shape[3:])

    q, k, v, g, beta = (chunks(a) for a in (q, k, v, g, beta))
    gcum = jnp.cumsum(g, axis=-1)
    idx = jnp.arange(CHUNK)
    incl = idx[:, None] >= idx[None, :]
    strict = idx[:, None] > idx[None, :]
    diff = gcum[..., :, None] - gcum[..., None, :]
    decay = jnp.where(incl, jnp.exp(jnp.where(incl, diff, 0.0)), 0.0)
    k_beta = k * beta[..., None]
    a_mat = jnp.where(strict, jnp.einsum('bhnid,bhnjd->bhnij', k_beta, k) * decay, 0.0)
    rhs = jnp.concatenate([v * beta[..., None], k_beta * jnp.exp(gcum)[..., None]], axis=-1)
    sol = lax.linalg.triangular_solve(a_mat, rhs, left_side=True, lower=True, unit_diagonal=True)
    u, w = sol[..., :dv], sol[..., dv:]
    qk = jnp.einsum('bhnid,bhnjd->bhnij', q, k) * decay
    q_dec = q * jnp.exp(gcum)[..., None]
    k_dec = k * jnp.exp(gcum[..., -1:] - gcum)[..., None]
    chunk_decay = jnp.exp(gcum[..., -1])

    def step(s, xs):
        u_n, w_n, qk_n, qd_n, kd_n, cd_n = xs
        v_new = u_n - jnp.einsum('bhck,bhkv->bhcv', w_n, s)
        o_n = jnp.einsum('bhck,bhkv->bhcv', qd_n, s) + jnp.einsum('bhij,bhjv->bhiv', qk_n, v_new)
        s = s * cd_n[..., None, None] + jnp.einsum('bhck,bhcv->bhkv', kd_n, v_new)
        return s, o_n

    xs = tuple(jnp.moveaxis(a, 2, 0) for a in (u, w, qk, q_dec, k_dec, chunk_decay))
    state, o = lax.scan(step, state, xs)
    o = jnp.moveaxis(o, 0, 2).reshape(b, nh, t_len, dv)
    return o, state


def _dn_prepare(p, conv_w, a_log, dt_bias):
    b, t, _ = p.shape
    qkv = jax.nn.silu(centred_conv(p[..., :3 * DN_WIDTH], conv_w))
    q, k, v = jnp.split(qkv, 3, axis=-1)
    q = l2_normalize(dn_heads(q)) * DN_HEAD_DIM ** -0.5
    k = l2_normalize(dn_heads(k))
    v = dn_heads(v)
    z = p[..., 3 * DN_WIDTH:4 * DN_WIDTH]
    gates = p[..., 4 * DN_WIDTH:].reshape(b, t, 4, DN_HEADS).transpose(2, 0, 3, 1)
    beta = jax.nn.sigmoid(gates[:2])
    g = -jnp.exp(a_log)[:, None, :, None] * jax.nn.softplus(gates[2:] + dt_bias[:, None, :, None])
    return q, k, v, z, beta, g


def _dn_output(o, z, out_g):
    b, _, t, _ = o.shape
    o = o.transpose(0, 2, 1, 3)
    o = o * lax.rsqrt(jnp.mean(o * o, axis=-1, keepdims=True) + NORM_EPS) * out_g
    o = o * jax.nn.silu(z.reshape(b, t, DN_HEADS, DN_HEAD_DIM))
    return o.reshape(b, t, DN_WIDTH)


def gated_deltanet_mixer(p_ctx, p_lat, conv_w, a_log, dt_bias, out_g):
    p_ctx = p_ctx.astype(jnp.float32)
    p_lat = p_lat.astype(jnp.float32)
    conv_w = conv_w.astype(jnp.float32)
    streams = (_dn_prepare(p_ctx, conv_w, a_log, dt_bias), _dn_prepare(p_lat, conv_w, a_log, dt_bias))
    b = p_ctx.shape[0]
    outs = [0.0, 0.0]
    for d in range(2):
        flip = (lambda a: jnp.flip(a, axis=2)) if d == 1 else (lambda a: a)
        s = jnp.zeros((b, DN_HEADS, DN_HEAD_DIM, DN_HEAD_DIM), jnp.float32)
        for i, (q, k, v, _, beta, g) in enumerate(streams):
            o, s = gated_delta_rule_chunked(flip(q), flip(k), flip(v), flip(g[d]), flip(beta[d]), s)
            outs[i] = outs[i] + flip(o)
    return (_dn_output(outs[0], streams[0][3], out_g), _dn_output(outs[1], streams[1][3], out_g))


def rw_heads(a):
    return a.reshape(*a.shape[:-1], RW_HEADS, RW_HEAD_DIM)


def quad_shift(p, rows):
    b, t, ch = p.shape
    gq = p.reshape(b, rows, GRID_W, ch // 4, 4)
    left = jnp.pad(gq[..., 0], ((0, 0), (0, 0), (1, 0), (0, 0)))[:, :, :-1]
    right = jnp.pad(gq[..., 1], ((0, 0), (0, 0), (0, 1), (0, 0)))[:, :, 1:]
    up = jnp.pad(gq[..., 2], ((0, 0), (1, 0), (0, 0), (0, 0)))[:, :-1]
    down = jnp.pad(gq[..., 3], ((0, 0), (0, 1), (0, 0), (0, 0)))[:, 1:]
    return jnp.stack([left, right, up, down], axis=-1).reshape(b, t, ch)


def bi_shift(p):
    b, t, ch = p.shape
    gq = p.reshape(b, t, ch // 2, 2)
    prev = jnp.pad(gq[..., 0], ((0, 0), (1, 0), (0, 0)))[:, :-1]
    nxt = jnp.pad(gq[..., 1], ((0, 0), (0, 1), (0, 0)))[:, 1:]
    return jnp.stack([prev, nxt], axis=-1).reshape(b, t, ch)


def rwkv7_scan(r, w, k, v, a, bvec, s0, reverse):
    def step(s, inp):
        r_t, w_t, k_t, v_t, a_t, b_t = inp
        sa = jnp.einsum('bhvk,bhk->bhv', s, a_t)
        s = s * w_t[:, :, None, :] + sa[..., None] * b_t[:, :, None, :] + v_t[..., None] * k_t[:, :, None, :]
        return s, jnp.einsum('bhvk,bhk->bhv', s, r_t)

    xs = tuple(jnp.moveaxis(a_, 1, 0) for a_ in (r, w, k, v, a, bvec))
    s, y = lax.scan(step, s0, xs, reverse=reverse)
    return jnp.moveaxis(y, 0, 1), s


def _rw_prepare(p, g_up, k_k):
    b, t, _ = p.shape
    r = p[..., :RW_WIDTH]
    k = p[..., RW_WIDTH:2 * RW_WIDTH]
    v = p[..., 2 * RW_WIDTH:3 * RW_WIDTH]
    off = 3 * RW_WIDTH
    wd = p[..., off:off + 2 * W_LORA].reshape(b, t, 2, W_LORA)
    off = off + 2 * W_LORA
    ad = p[..., off:off + 2 * A_LORA].reshape(b, t, 2, A_LORA)
    gate = jax.nn.sigmoid(p[..., off + 2 * A_LORA:]) @ g_up
    kk = l2_normalize(rw_heads(k * k_k))
    return r, k, v, wd, ad, gate, kk


def _rw_direction(inp, d, s0, w0, w_up, a0, a_up, k_a, r_k):
    r, k, v, wd, ad, _, kk = inp
    log_w = -jax.nn.softplus(-(w0[d] + jnp.tanh(wd[:, :, d]) @ w_up[d])) - 0.5
    decay = jnp.exp(-jnp.exp(log_w))
    iclr = jax.nn.sigmoid(a0[d] + ad[:, :, d] @ a_up[d])
    k_d = rw_heads(k * (1.0 + (iclr - 1.0) * k_a))
    r_h, v_h = rw_heads(r), rw_heads(v)
    y, s = rwkv7_scan(r_h, rw_heads(decay), k_d, v_h, -kk, kk * rw_heads(iclr), s0, reverse=(d == 1))
    bonus = jnp.sum(r_h * k_d * r_k, axis=-1, keepdims=True) * v_h
    return y, bonus, s


def rw_group_norm(y, ln_w, ln_b):
    mean = jnp.mean(y, axis=-1, keepdims=True)
    var = jnp.mean(jnp.square(y - mean), axis=-1, keepdims=True)
    yn = (y - mean) * lax.rsqrt(var + RW_LN_EPS)
    return yn.reshape(*y.shape[:-2], RW_WIDTH) * ln_w + ln_b


def rwkv7_mixer(p_ctx, p_lat, rows, mu, w0, w_up, a0, a_up, g_up, k_k, k_a, r_k, ln_w, ln_b):
    p_ctx = p_ctx.astype(jnp.float32)
    p_lat = p_lat.astype(jnp.float32)
    p_ctx = p_ctx + mu * (bi_shift(p_ctx) - p_ctx)
    p_lat = p_lat + mu * (quad_shift(p_lat, rows) - p_lat)
    streams = (_rw_prepare(p_ctx, g_up, k_k), _rw_prepare(p_lat, g_up, k_k))
    b = p_ctx.shape[0]
    ys = [0.0, 0.0]
    bonuses = [0.0, 0.0]
    for d in range(2):
        s = jnp.zeros((b, RW_HEADS, RW_HEAD_DIM, RW_HEAD_DIM), jnp.float32)
        for i, inp in enumerate(streams):
            y, bonus, s = _rw_direction(inp, d, s, w0, w_up, a0, a_up, k_a, r_k)
            ys[i] = ys[i] + y
            bonuses[i] = bonuses[i] + bonus
    outs = []
    for i, inp in enumerate(streams):
        bonus_flat = bonuses[i].reshape(*bonuses[i].shape[:-2], RW_WIDTH)
        outs.append((rw_group_norm(ys[i], ln_w, ln_b) + bonus_flat) * inp[5])
    return outs[0], outs[1]


def mixer_residual(h, o, gate, w_o, g_post):
    return h + gate * rms_norm(o.astype(h.dtype) @ w_o, g_post)


def ffn_residual(h, shift, scale, gate, g_pre, g_post, w_i, w_o):
    u = modulate(rms_norm(h, g_pre), shift, scale)
    gt, up = jnp.split(u @ w_i, 2, axis=-1)
    return h + gate * rms_norm((jax.nn.silu(gt) * up) @ w_o, g_post)


def setup_inputs(seed: int = 0) -> dict:
    key = jax.random.key(seed)
    ks = iter(jax.random.split(key, 32))
    L = DEPTH

    def nrm(shape, scale):
        return jax.random.normal(next(ks), shape, jnp.float32) * scale

    def uni(shape, lo, hi):
        return jax.random.uniform(next(ks), shape, jnp.float32, lo, hi)

    dt = jnp.exp(uni((L, 2, DN_HEADS), math.log(1e-3), math.log(1e-1)))
    return {
        'x': nrm((BATCH, SEQ, D_MODEL), 1.0),
        'c': nrm((BATCH, D_MODEL), 1.0),
        'ctx': nrm((BATCH, CTX_LEN, D_MODEL), 1.0),
        'c_ctx': nrm((D_MODEL,), 1.0),
        'w_mod': nrm((L, D_MODEL, 6 * D_MODEL), 0.5 * D_MODEL ** -0.5),
        'b_mod': nrm((L, 6 * D_MODEL), 0.02),
        'mix_pre_g': 1.0 + nrm((L, D_MODEL), 0.1),
        'mix_post_g': 1.0 + nrm((L, D_MODEL), 0.1),
        'ffn_pre_g': 1.0 + nrm((L, D_MODEL), 0.1),
        'ffn_post_g': 1.0 + nrm((L, D_MODEL), 0.1),
        'w_in': nrm((L, D_MODEL, P_IN), D_MODEL ** -0.5),
        'dn_conv': nrm((L, CONV_K, 3 * DN_WIDTH), CONV_K ** -0.5),
        'dn_a_log': jnp.log(uni((L, 2, DN_HEADS), 1.0, 16.0)),
        'dn_dt_bias': dt + jnp.log(-jnp.expm1(-dt)),
        'dn_out_g': 1.0 + nrm((L, DN_HEAD_DIM), 0.1),
        'rw_mu': uni((L, P_RW), 0.0, 1.0),
        'rw_w0': uni((L, 2, RW_WIDTH), -6.0, 1.0),
        'rw_w_up': nrm((L, 2, W_LORA, RW_WIDTH), 0.5 * W_LORA ** -0.5),
        'rw_a0': nrm((L, 2, RW_WIDTH), 0.1),
        'rw_a_up': nrm((L, 2, A_LORA, RW_WIDTH), 0.5 * A_LORA ** -0.5),
        'rw_g_up': nrm((L, G_LORA, RW_WIDTH), G_LORA ** -0.5),
        'rw_k_k': 0.85 + nrm((L, RW_WIDTH), 0.1),
        'rw_k_a': 1.0 + nrm((L, RW_WIDTH), 0.1),
        'rw_r_k': nrm((L, RW_HEADS, RW_HEAD_DIM), 0.1),
        'rw_ln_w': 1.0 + nrm((L, RW_WIDTH), 0.1),
        'rw_ln_b': nrm((L, RW_WIDTH), 0.02),
        'w_out': nrm((L, D_MODEL, D_MODEL), D_MODEL ** -0.5),
        'w_ffn_in': nrm((L, D_MODEL, 2 * FFN_HIDDEN), D_MODEL ** -0.5),
        'w_ffn_out': nrm((L, FFN_HIDDEN, D_MODEL), FFN_HIDDEN ** -0.5),
    }


def reference(x, c, ctx, c_ctx, w_mod, b_mod, mix_pre_g, mix_post_g, ffn_pre_g, ffn_post_g,
              w_in, dn_conv, dn_a_log, dn_dt_bias, dn_out_g, rw_mu, rw_w0, rw_w_up, rw_a0,
              rw_a_up, rw_g_up, rw_k_k, rw_k_a, rw_r_k, rw_ln_w, rw_ln_b, w_out, w_ffn_in,
              w_ffn_out):
    rows = x.shape[1] // GRID_W
    n_ctx = ctx.shape[1]
    silu_c = jax.nn.silu(c)[:, None, :]
    silu_cc = jax.nn.silu(c_ctx)
    h, h_ctx = x, ctx
    for l in range(DEPTH):
        last = l == DEPTH - 1
        mod_lat = jnp.split(silu_c @ w_mod[l] + b_mod[l], 6, axis=-1)
        mod_ctx = jnp.split(silu_cc @ w_mod[l] + b_mod[l], 6, axis=-1)
        u_lat = modulate(rms_norm(h, mix_pre_g[l]), mod_lat[0], mod_lat[1])
        u_ctx = modulate(rms_norm(h_ctx, mix_pre_g[l]), mod_ctx[0], mod_ctx[1])
        p = jnp.concatenate([u_ctx, u_lat], axis=1) @ w_in[l]
        p_ctx, p_lat = p[:, :n_ctx], p[:, n_ctx:]
        dn_ctx, dn_lat = gated_deltanet_mixer(
            p_ctx[..., :P_DN], to_col_major(p_lat[..., :P_DN], rows),
            dn_conv[l], dn_a_log[l], dn_dt_bias[l], dn_out_g[l])
        dn_lat = to_row_major(dn_lat, rows)
        rw_ctx, rw_lat = rwkv7_mixer(
            p_ctx[..., P_DN:], p_lat[..., P_DN:], rows, rw_mu[l], rw_w0[l], rw_w_up[l],
            rw_a0[l], rw_a_up[l], rw_g_up[l], rw_k_k[l], rw_k_a[l], rw_r_k[l], rw_ln_w[l], rw_ln_b[l])
        h = mixer_residual(h, jnp.concatenate([dn_lat, rw_lat], axis=-1), mod_lat[2], w_out[l], mix_post_g[l])
        h = ffn_residual(h, mod_lat[3], mod_lat[4], mod_lat[5], ffn_pre_g[l], ffn_post_g[l],
                         w_ffn_in[l], w_ffn_out[l])
        if not last:
            h_ctx = mixer_residual(h_ctx, jnp.concatenate([dn_ctx, rw_ctx], axis=-1), mod_ctx[2],
                                   w_out[l], mix_post_g[l])
            h_ctx = ffn_residual(h_ctx, mod_ctx[3], mod_ctx[4], mod_ctx[5], ffn_pre_g[l],
                                 ffn_post_g[l], w_ffn_in[l], w_ffn_out[l])
    return h
```

```python
import functools

import jax
import jax.numpy as jnp
from jax import lax
from jax.experimental import pallas as pl
from jax.experimental.pallas import tpu as pltpu

D_MODEL = 2048
DEPTH = 4
GRID_W = 64
DN_WIDTH = 1024
DN_HEAD_DIM = 128
DN_HEADS = 8
RW_WIDTH = 1024
RW_HEAD_DIM = 64
RW_HEADS = 16
CONV_K = 7
CHUNK = 64
W_LORA = 64
A_LORA = 64
G_LORA = 160
FFN_HIDDEN = 5632
P_DN = 4 * DN_WIDTH + 4 * DN_HEADS
P_RW = 3 * RW_WIDTH + 2 * W_LORA + 2 * A_LORA + G_LORA
NORM_EPS = 1e-6
RW_LN_EPS = 64e-5

SMALL_W = 512
TRI_BLOCK = 16
VMEM_LIMIT = 48 * 1024 * 1024

F32 = jnp.float32
BF16 = jnp.bfloat16


def _mm(a, b):
    return jnp.dot(a.astype(BF16), b.astype(BF16), preferred_element_type=F32)


def _mm_nt(a, b):
    return lax.dot_general(a.astype(BF16), b.astype(BF16), (((1,), (1,)), ((), ())),
                           preferred_element_type=F32)


def _mm_tn(a, b):
    return lax.dot_general(a.astype(BF16), b.astype(BF16), (((0,), (0,)), ((), ())),
                           preferred_element_type=F32)


def _split3(x):
    hi = x.astype(BF16)
    r1 = x - hi.astype(F32)
    mid = r1.astype(BF16)
    lo = (r1 - mid.astype(F32)).astype(BF16)
    return hi, mid, lo


def _mm_exact_lhs(a_bf16, b):
    hi, mid, lo = _split3(b)
    return (jnp.dot(a_bf16, hi, preferred_element_type=F32)
            + jnp.dot(a_bf16, mid, preferred_element_type=F32)
            + jnp.dot(a_bf16, lo, preferred_element_type=F32))


def _mm_exact_rhs(a, b_bf16):
    hi, mid, lo = _split3(a)
    return (jnp.dot(hi, b_bf16, preferred_element_type=F32)
            + jnp.dot(mid, b_bf16, preferred_element_type=F32)
            + jnp.dot(lo, b_bf16, preferred_element_type=F32))


def _mm_hp(a, b):
    a_hi = a.astype(BF16)
    a_lo = (a - a_hi.astype(F32)).astype(BF16)
    b_hi = b.astype(BF16)
    b_lo = (b - b_hi.astype(F32)).astype(BF16)
    return (jnp.dot(a_hi, b_hi, preferred_element_type=F32)
            + jnp.dot(a_hi, b_lo, preferred_element_type=F32)
            + jnp.dot(a_lo, b_hi, preferred_element_type=F32))


def _chunk_masks(reverse):
    row = lax.broadcasted_iota(jnp.int32, (CHUNK, CHUNK), 0)
    col = lax.broadcasted_iota(jnp.int32, (CHUNK, CHUNK), 1)
    if reverse:
        strict, incl = row < col, row <= col
    else:
        strict, incl = row > col, row >= col
    same_block = (row // TRI_BLOCK) == (col // TRI_BLOCK)
    eye = (row == col).astype(F32)
    return strict, incl, same_block, eye


def _cumsum_mats(reverse):
    row = lax.broadcasted_iota(jnp.int32, (CHUNK, CHUNK), 0)
    col = lax.broadcasted_iota(jnp.int32, (CHUNK, CHUNK), 1)
    lower, upper = (row >= col).astype(BF16), (row <= col).astype(BF16)
    return (upper, lower) if reverse else (lower, upper)


def _unit_tri_inverse(a, same_block, eye):
    a_d = jnp.where(same_block, a, 0.0)
    a_o = a - a_d
    x = eye - a_d
    p = _mm_hp(a_d, a_d)
    x = x + _mm_hp(x, p)
    p = _mm_hp(p, p)
    x = x + _mm_hp(x, p)
    p = _mm_hp(p, p)
    dinv = x + _mm_hp(x, p)
    n = _mm_hp(dinv, a_o)
    n2 = _mm_hp(n, n)
    y = eye - n
    y = y + _mm_hp(y, n2)
    return _mm_hp(y, dinv)


def _mm_kernel(x_ref, w_ref, o_ref):
    o_ref[...] = _mm(x_ref[...], w_ref[...]).astype(o_ref.dtype)


def _matmul(x, w, *, tm, tn, out_dtype=F32, name="matmul"):
    m, k = x.shape
    n = w.shape[1]
    assert m % tm == 0 and n % tn == 0, (m, n, tm, tn)
    return pl.pallas_call(
        _mm_kernel,
        grid=(m // tm, n // tn),
        in_specs=[pl.BlockSpec((tm, k), lambda i, j: (i, 0)),
                  pl.BlockSpec((k, tn), lambda i, j: (0, j))],
        out_specs=pl.BlockSpec((tm, tn), lambda i, j: (i, j)),
        out_shape=jax.ShapeDtypeStruct((m, n), out_dtype),
        compiler_params=pltpu.CompilerParams(
            dimension_semantics=("parallel", "arbitrary"), vmem_limit_bytes=VMEM_LIMIT),
        name=name,
    )(x, w)


def _swiglu_kernel(x_ref, wg_ref, wu_ref, o_ref):
    x = x_ref[...]
    g = _mm(x, wg_ref[...])
    u = _mm(x, wu_ref[...])
    o_ref[...] = (g * jax.nn.sigmoid(g) * u).astype(o_ref.dtype)


def _swiglu_matmul(x, wg, wu, *, tm, tn):
    m, k = x.shape
    n = wg.shape[1]
    assert m % tm == 0 and n % tn == 0
    return pl.pallas_call(
        _swiglu_kernel,
        grid=(m // tm, n // tn),
        in_specs=[pl.BlockSpec((tm, k), lambda i, j: (i, 0)),
                  pl.BlockSpec((k, tn), lambda i, j: (0, j)),
                  pl.BlockSpec((k, tn), lambda i, j: (0, j))],
        out_specs=pl.BlockSpec((tm, tn), lambda i, j: (i, j)),
        out_shape=jax.ShapeDtypeStruct((m, n), BF16),
        compiler_params=pltpu.CompilerParams(
            dimension_semantics=("parallel", "arbitrary"), vmem_limit_bytes=VMEM_LIMIT),
        name="ffn_in_swiglu",
    )(x, wg, wu)


def _chunk_order(n, n_ctx_chunks, n_chunks, reverse):
    if not reverse:
        return n
    return jnp.where(n < n_ctx_chunks, n_ctx_chunks - 1 - n, n_chunks + n_ctx_chunks - 1 - n)


def _dn_kernel(q_ref, k_ref, v_ref, gcol_ref, grow_ref, bcol_ref, o_ref, s_ref, *, reverse):
    @pl.when(pl.program_id(1) == 0)
    def _():
        s_ref[...] = jnp.zeros_like(s_ref)

    strict, incl, same_block, eye = _chunk_masks(reverse)
    tri, tri_t = _cumsum_mats(reverse)
    gcol = gcol_ref[0, 0]
    grow = grow_ref[0, 0]
    beta_all = bcol_ref[0, 0]
    gcum_col = _mm_exact_lhs(tri, gcol)
    gcum_row = _mm_exact_rhs(grow, tri_t)
    gtot_col = jnp.sum(gcol, axis=0, keepdims=True)

    for h in range(DN_HEADS):
        sl = slice(h * DN_HEAD_DIM, (h + 1) * DN_HEAD_DIM)
        qh = q_ref[0, :, sl]
        kh = k_ref[0, :, sl]
        vh = v_ref[0, :, sl]
        gc = gcum_col[:, h:h + 1]
        gr = gcum_row[h:h + 1, :]
        beta = beta_all[:, h:h + 1]
        gt = gtot_col[:, h:h + 1]
        dec = jnp.where(incl, jnp.exp(jnp.where(incl, gc - gr, 0.0)), 0.0)
        kb = kh * beta
        a_mat = jnp.where(strict, _mm_nt(kb, kh) * dec, 0.0)
        t_inv = _unit_tri_inverse(a_mat, same_block, eye)
        eg = jnp.exp(gc)
        rhs = jnp.concatenate([vh * beta, kb * eg], axis=1)
        sol = _mm(t_inv, rhs)
        u = sol[:, :DN_HEAD_DIM]
        w = sol[:, DN_HEAD_DIM:]
        qk = jnp.where(incl, _mm_nt(qh, kh) * dec, 0.0)
        q_dec = qh * eg
        k_dec = kh * jnp.exp(gt - gc)
        s = s_ref[h]
        v_new = u - _mm(w, s)
        o_ref[0, :, sl] = _mm(q_dec, s) + _mm(qk, v_new)
        s_ref[h] = s * jnp.exp(gt) + _mm_tn(k_dec, v_new)


def _dn_scan(q, k, v, gcol, grow, bcol, *, n_ctx_chunks, reverse):
    b, t, _ = q.shape
    n_chunks = t // CHUNK
    order = functools.partial(_chunk_order, n_ctx_chunks=n_ctx_chunks, n_chunks=n_chunks,
                              reverse=reverse)
    tok_spec = pl.BlockSpec((1, CHUNK, DN_WIDTH), lambda i, n: (i, order(n), 0))
    col_spec = pl.BlockSpec((1, 1, CHUNK, DN_HEADS), lambda i, n: (i, order(n), 0, 0))
    row_spec = pl.BlockSpec((1, 1, DN_HEADS, CHUNK), lambda i, n: (i, order(n), 0, 0))
    return pl.pallas_call(
        functools.partial(_dn_kernel, reverse=reverse),
        grid=(b, n_chunks),
        in_specs=[tok_spec, tok_spec, tok_spec, col_spec, row_spec, col_spec],
        out_specs=tok_spec,
        out_shape=jax.ShapeDtypeStruct((b, t, DN_WIDTH), F32),
        scratch_shapes=[pltpu.VMEM((DN_HEADS, DN_HEAD_DIM, DN_HEAD_DIM), F32)],
        compiler_params=pltpu.CompilerParams(
            dimension_semantics=("parallel", "arbitrary"), vmem_limit_bytes=VMEM_LIMIT),
        name="dn_scan_rev" if reverse else "dn_scan_fwd",
    )(q, k, v, gcol, grow, bcol)


def _rw_kernel(r_ref, lw_ref, k_ref, v_ref, a_ref, b_ref, y_ref, s_ref, *, reverse):
    @pl.when(pl.program_id(1) == 0)
    def _():
        s_ref[...] = jnp.zeros_like(s_ref)

    strict, incl, same_block, eye = _chunk_masks(reverse)
    tri, _ = _cumsum_mats(reverse)
    lw = lw_ref[0]
    cum = _mm_exact_lhs(tri, lw)
    tot = jnp.sum(lw, axis=0, keepdims=True)
    w_cum = jnp.exp(cum)
    w_inv = jnp.exp(-cum)
    w_prev = jnp.exp(cum - lw)
    w_rest = jnp.exp(tot - cum)
    w_tot = jnp.exp(tot)
    a_t = a_ref[0] * w_prev
    r_t = r_ref[0] * w_cum
    b_all = b_ref[0]
    k_all = k_ref[0]
    b_t = b_all * w_inv
    k_t = k_all * w_inv
    b_e = b_all * w_rest
    k_e = k_all * w_rest

    for h in range(RW_HEADS):
        sl = slice(h * RW_HEAD_DIM, (h + 1) * RW_HEAD_DIM)
        ah = a_t[:, sl]
        rh = r_t[:, sl]
        vh = v_ref[0, :, sl]
        g = _mm_nt(jnp.concatenate([ah, rh], axis=0),
                   jnp.concatenate([b_t[:, sl], k_t[:, sl]], axis=0))
        a_ab = jnp.where(strict, g[:CHUNK, :CHUNK], 0.0)
        a_ak = jnp.where(strict, g[:CHUNK, CHUNK:], 0.0)
        m_rb = jnp.where(incl, g[CHUNK:, :CHUNK], 0.0)
        m_rk = jnp.where(incl, g[CHUNK:, CHUNK:], 0.0)
        t_inv = _unit_tri_inverse(-a_ab, same_block, eye)
        sol = _mm(t_inv, jnp.concatenate([ah, _mm(a_ak, vh)], axis=1))
        a2 = sol[:, :RW_HEAD_DIM]
        v2 = sol[:, RW_HEAD_DIM:]
        s = s_ref[h]
        lh = _mm_nt(jnp.concatenate([a2, rh], axis=0), s)
        u = lh[:CHUNK] + v2
        uv = jnp.concatenate([u, vh], axis=0)
        y_ref[0, :, sl] = lh[CHUNK:] + _mm(jnp.concatenate([m_rb, m_rk], axis=1), uv)
        bk = jnp.concatenate([b_e[:, sl], k_e[:, sl]], axis=0)
        s_ref[h] = s * w_tot[:, sl] + _mm_tn(uv, bk)


def _rw_scan(r, lw, k, v, a, bvec, *, n_ctx_chunks, reverse):
    b, t, _ = r.shape
    n_chunks = t // CHUNK
    order = functools.partial(_chunk_order, n_ctx_chunks=n_ctx_chunks, n_chunks=n_chunks,
                              reverse=reverse)
    tok_spec = pl.BlockSpec((1, CHUNK, RW_WIDTH), lambda i, n: (i, order(n), 0))
    return pl.pallas_call(
        functools.partial(_rw_kernel, reverse=reverse),
        grid=(b, n_chunks),
        in_specs=[tok_spec] * 6,
        out_specs=tok_spec,
        out_shape=jax.ShapeDtypeStruct((b, t, RW_WIDTH), F32),
        scratch_shapes=[pltpu.VMEM((RW_HEADS, RW_HEAD_DIM, RW_HEAD_DIM), F32)],
        compiler_params=pltpu.CompilerParams(
            dimension_semantics=("parallel", "arbitrary"), vmem_limit_bytes=VMEM_LIMIT),
        name="rw_scan_rev" if reverse else "rw_scan_fwd",
    )(r, lw, k, v, a, bvec)


def _rms(x, gain):
    return x * lax.rsqrt(jnp.mean(x * x, axis=-1, keepdims=True) + NORM_EPS) * gain


def _to_col_major(a, rows):
    b, t, ch = a.shape
    return a.reshape(b, rows, GRID_W, ch).transpose(0, 2, 1, 3).reshape(b, t, ch)


def _to_row_major(a, rows):
    b, t, ch = a.shape
    return a.reshape(b, GRID_W, rows, ch).transpose(0, 2, 1, 3).reshape(b, t, ch)


def _centred_conv(a, w):
    t = a.shape[1]
    half = CONV_K // 2
    ap = jnp.pad(a, ((0, 0), (half, half), (0, 0)))
    out = ap[:, 0:t] * w[0]
    for j in range(1, CONV_K):
        out = out + ap[:, j:j + t] * w[j]
    return out


def _quad_shift(p, rows):
    b, t, ch = p.shape
    gq = p.reshape(b, rows, GRID_W, ch // 4, 4)
    left = jnp.pad(gq[..., 0], ((0, 0), (0, 0), (1, 0), (0, 0)))[:, :, :-1]
    right = jnp.pad(gq[..., 1], ((0, 0), (0, 0), (0, 1), (0, 0)))[:, :, 1:]
    up = jnp.pad(gq[..., 2], ((0, 0), (1, 0), (0, 0), (0, 0)))[:, :-1]
    down = jnp.pad(gq[..., 3], ((0, 0), (0, 1), (0, 0), (0, 0)))[:, 1:]
    return jnp.stack([left, right, up, down], axis=-1).reshape(b, t, ch)


def _bi_shift(p):
    b, t, ch = p.shape
    gq = p.reshape(b, t, ch // 2, 2)
    prev = jnp.pad(gq[..., 0], ((0, 0), (1, 0), (0, 0)))[:, :-1]
    nxt = jnp.pad(gq[..., 1], ((0, 0), (0, 1), (0, 0)))[:, 1:]
    return jnp.stack([prev, nxt], axis=-1).reshape(b, t, ch)


def _l2n(x):
    return x * lax.rsqrt(jnp.sum(x * x, axis=-1, keepdims=True) + NORM_EPS)


def _permute_w_in(w):
    big_dn = w[:, :4 * DN_WIDTH]
    big_rw = w[:, P_DN:P_DN + 3 * RW_WIDTH]
    small = jnp.concatenate([w[:, 4 * DN_WIDTH:P_DN], w[:, P_DN + 3 * RW_WIDTH:]], axis=1)
    small = jnp.pad(small, ((0, 0), (0, SMALL_W - small.shape[1])))
    return jnp.concatenate([big_dn, big_rw, small], axis=1)


def _lora_weight(w_up, a_up, g_up):
    z = jnp.zeros
    rows = [
        jnp.concatenate([w_up[0], z((W_LORA, 4 * RW_WIDTH), F32)], axis=1),
        jnp.concatenate([z((W_LORA, RW_WIDTH), F32), w_up[1], z((W_LORA, 3 * RW_WIDTH), F32)], axis=1),
        jnp.concatenate([z((A_LORA, 2 * RW_WIDTH), F32), a_up[0], z((A_LORA, 2 * RW_WIDTH), F32)], axis=1),
        jnp.concatenate([z((A_LORA, 3 * RW_WIDTH), F32), a_up[1], z((A_LORA, RW_WIDTH), F32)], axis=1),
        jnp.concatenate([z((G_LORA, 4 * RW_WIDTH), F32), g_up], axis=1),
        z((SMALL_W - 2 * W_LORA - 2 * A_LORA - G_LORA, 5 * RW_WIDTH), F32),
    ]
    return jnp.concatenate(rows, axis=0)


def kernel(x, c, ctx, c_ctx, w_mod, b_mod, mix_pre_g, mix_post_g, ffn_pre_g, ffn_post_g, w_in,
           dn_conv, dn_a_log, dn_dt_bias, dn_out_g, rw_mu, rw_w0, rw_w_up, rw_a0, rw_a_up, rw_g_up,
           rw_k_k, rw_k_a, rw_r_k, rw_ln_w, rw_ln_b, w_out, w_ffn_in, w_ffn_out):
    bsz, seq, d = x.shape
    n_ctx = ctx.shape[1]
    rows = seq // GRID_W
    t_all = n_ctx + seq
    m_all = bsz * t_all
    n_ctx_chunks = n_ctx // CHUNK
    n_chunks = t_all // CHUNK
    tm = 512

    cond = jnp.concatenate([jax.nn.silu(c), jax.nn.silu(c_ctx)[None]], axis=0)
    cond = jnp.pad(cond, ((0, 8 - cond.shape[0]), (0, 0)))

    h = jnp.concatenate([ctx, x], axis=1)

    def per_row(mod_l, idx):
        lat = mod_l[:bsz, idx][:, None, :]
        cx = jnp.broadcast_to(mod_l[bsz, idx][None, None, :], (bsz, 1, d))
        return jnp.concatenate([jnp.broadcast_to(cx, (bsz, n_ctx, d)),
                                jnp.broadcast_to(lat, (bsz, seq, d))], axis=1)

    for l in range(DEPTH):
        mod_l = _matmul(cond, w_mod[l], tm=8, tn=1536, name="adaln") + b_mod[l]
        mod_l = mod_l.reshape(8, 6, d)

        u = _rms(h, mix_pre_g[l]) * (1.0 + per_row(mod_l, 1)) + per_row(mod_l, 0)
        w_in_l = _permute_w_in(w_in[l]).astype(BF16)
        p = _matmul(u.reshape(m_all, d).astype(BF16), w_in_l, tm=tm, tn=768, name="in_proj")
        p = p.reshape(bsz, t_all, -1)
        p_dn_big = p[..., :4 * DN_WIDTH]
        p_rw_big = p[..., 4 * DN_WIDTH:4 * DN_WIDTH + 3 * RW_WIDTH]
        p_small = p[..., 4 * DN_WIDTH + 3 * RW_WIDTH:]
        dn_gates = p_small[..., :4 * DN_HEADS]
        rw_small = p_small[..., 4 * DN_HEADS:4 * DN_HEADS + P_RW - 3 * RW_WIDTH]

        dn_in = jnp.concatenate([p_dn_big, dn_gates], axis=-1)
        dn_in = jnp.concatenate([dn_in[:, :n_ctx], _to_col_major(dn_in[:, n_ctx:], rows)], axis=1)
        conv_w = dn_conv[l]
        qkv = jnp.concatenate([_centred_conv(dn_in[:, :n_ctx, :3 * DN_WIDTH], conv_w),
                               _centred_conv(dn_in[:, n_ctx:, :3 * DN_WIDTH], conv_w)], axis=1)
        qkv = jax.nn.silu(qkv)
        q, k, v = jnp.split(qkv, 3, axis=-1)
        hd = lambda a: a.reshape(bsz, t_all, DN_HEADS, DN_HEAD_DIM)
        q = (_l2n(hd(q)) * DN_HEAD_DIM ** -0.5).reshape(bsz, t_all, DN_WIDTH)
        k = _l2n(hd(k)).reshape(bsz, t_all, DN_WIDTH)
        z = dn_in[..., 3 * DN_WIDTH:4 * DN_WIDTH]
        gates = dn_in[..., 4 * DN_WIDTH:].reshape(bsz, t_all, 4, DN_HEADS)
        beta = jax.nn.sigmoid(gates[:, :, :2])
        g = -jnp.exp(dn_a_log[l])[None, None] * jax.nn.softplus(gates[:, :, 2:] + dn_dt_bias[l][None, None])
        dn_o = 0.0
        for dr in range(2):
            gcol = g[:, :, dr].reshape(bsz, n_chunks, CHUNK, DN_HEADS)
            grow = gcol.transpose(0, 1, 3, 2)
            bcol = beta[:, :, dr].reshape(bsz, n_chunks, CHUNK, DN_HEADS)
            dn_o = dn_o + _dn_scan(q, k, v, gcol, grow, bcol, n_ctx_chunks=n_ctx_chunks,
                                   reverse=(dr == 1))
        dn_o = hd(dn_o)
        dn_o = dn_o * lax.rsqrt(jnp.mean(dn_o * dn_o, axis=-1, keepdims=True) + NORM_EPS) * dn_out_g[l]
        dn_o = (dn_o * jax.nn.silu(hd(z))).reshape(bsz, t_all, DN_WIDTH)
        dn_o = jnp.concatenate([dn_o[:, :n_ctx], _to_row_major(dn_o[:, n_ctx:], rows)], axis=1)

        rw_in = jnp.concatenate([p_rw_big, rw_small], axis=-1)
        shifted = jnp.concatenate([_bi_shift(rw_in[:, :n_ctx]), _quad_shift(rw_in[:, n_ctx:], rows)], axis=1)
        rw_in = rw_in + rw_mu[l] * (shifted - rw_in)
        r = rw_in[..., :RW_WIDTH]
        kr = rw_in[..., RW_WIDTH:2 * RW_WIDTH]
        vr = rw_in[..., 2 * RW_WIDTH:3 * RW_WIDTH]
        off = 3 * RW_WIDTH
        wd = rw_in[..., off:off + 2 * W_LORA]
        ad = rw_in[..., off + 2 * W_LORA:off + 2 * W_LORA + 2 * A_LORA]
        gd = rw_in[..., off + 2 * W_LORA + 2 * A_LORA:]
        lora_in = jnp.concatenate([jnp.tanh(wd), ad, jax.nn.sigmoid(gd)], axis=-1)
        lora_in = jnp.pad(lora_in, ((0, 0), (0, 0), (0, SMALL_W - lora_in.shape[-1])))
        lora_w = _lora_weight(rw_w_up[l], rw_a_up[l], rw_g_up[l]).astype(BF16)
        lora = _matmul(lora_in.reshape(m_all, SMALL_W).astype(BF16), lora_w, tm=tm, tn=1024,
                       name="rw_lora").reshape(bsz, t_all, 5 * RW_WIDTH)
        gate = lora[..., 4 * RW_WIDTH:]
        rh = lambda a: a.reshape(bsz, t_all, RW_HEADS, RW_HEAD_DIM)
        kk = _l2n(rh(kr * rw_k_k[l])).reshape(bsz, t_all, RW_WIDTH)
        rw_y = 0.0
        bonus = 0.0
        for dr in range(2):
            log_w = -jax.nn.softplus(-(rw_w0[l, dr] + lora[..., dr * RW_WIDTH:(dr + 1) * RW_WIDTH])) - 0.5
            log_decay = -jnp.exp(log_w)
            iclr = jax.nn.sigmoid(rw_a0[l, dr] + lora[..., (2 + dr) * RW_WIDTH:(3 + dr) * RW_WIDTH])
            k_d = kr * (1.0 + (iclr - 1.0) * rw_k_a[l])
            rw_y = rw_y + _rw_scan(r, log_decay, k_d, vr, -kk, kk * iclr, n_ctx_chunks=n_ctx_chunks,
                                   reverse=(dr == 1))
            bonus = bonus + jnp.sum(rh(r) * rh(k_d) * rw_r_k[l], axis=-1, keepdims=True) * rh(vr)
        y = rh(rw_y)
        mean = jnp.mean(y, axis=-1, keepdims=True)
        var = jnp.mean(jnp.square(y - mean), axis=-1, keepdims=True)
        yn = ((y - mean) * lax.rsqrt(var + RW_LN_EPS)).reshape(bsz, t_all, RW_WIDTH)
        rw_o = (yn * rw_ln_w[l] + rw_ln_b[l] + bonus.reshape(bsz, t_all, RW_WIDTH)) * gate

        mix = jnp.concatenate([dn_o, rw_o], axis=-1).reshape(m_all, d).astype(BF16)
        mo = _matmul(mix, w_out[l].astype(BF16), tm=tm, tn=1024, name="out_proj").reshape(bsz, t_all, d)
        h = h + per_row(mod_l, 2) * _rms(mo, mix_post_g[l])

        u = _rms(h, ffn_pre_g[l]) * (1.0 + per_row(mod_l, 4)) + per_row(mod_l, 3)
        w_i = w_ffn_in[l].astype(BF16)
        hid = _swiglu_matmul(u.reshape(m_all, d).astype(BF16), w_i[:, :FFN_HIDDEN], w_i[:, FFN_HIDDEN:],
                             tm=tm, tn=512)
        fo = _matmul(hid, w_ffn_out[l].astype(BF16), tm=tm, tn=512, name="ffn_out").reshape(bsz, t_all, d)
        h = h + per_row(mod_l, 5) * _rms(fo, ffn_post_g[l])

    return h[:, n_ctx:]
```

```python
import functools

import jax
import jax.numpy as jnp
from jax import lax
from jax.experimental import pallas as pl
from jax.experimental.pallas import tpu as pltpu

D_MODEL = 2048
DEPTH = 4
GRID_W = 64
DN_WIDTH = 1024
DN_HEAD_DIM = 128
DN_HEADS = 8
RW_WIDTH = 1024
RW_HEAD_DIM = 64
RW_HEADS = 16
CONV_K = 7
CHUNK = 64
W_LORA = 64
A_LORA = 64
G_LORA = 160
FFN_HIDDEN = 5632
P_DN = 4 * DN_WIDTH + 4 * DN_HEADS
P_RW = 3 * RW_WIDTH + 2 * W_LORA + 2 * A_LORA + G_LORA
NORM_EPS = 1e-6
RW_LN_EPS = 64e-5

SMALL_W = 512
TRI_BLOCK = 16
VMEM_LIMIT = 48 * 1024 * 1024

F32 = jnp.float32
BF16 = jnp.bfloat16


def _mm(a, b):
    return jnp.dot(a.astype(BF16), b.astype(BF16), preferred_element_type=F32)


def _mm_nt(a, b):
    return lax.dot_general(a.astype(BF16), b.astype(BF16), (((1,), (1,)), ((), ())),
                           preferred_element_type=F32)


def _mm_tn(a, b):
    return lax.dot_general(a.astype(BF16), b.astype(BF16), (((0,), (0,)), ((), ())),
                           preferred_element_type=F32)


def _split3(x):
    hi = x.astype(BF16)
    r1 = x - hi.astype(F32)
    mid = r1.astype(BF16)
    lo = (r1 - mid.astype(F32)).astype(BF16)
    return hi, mid, lo


def _mm_exact_lhs(a_bf16, b):
    hi, mid, lo = _split3(b)
    return (jnp.dot(a_bf16, hi, preferred_element_type=F32)
            + jnp.dot(a_bf16, mid, preferred_element_type=F32)
            + jnp.dot(a_bf16, lo, preferred_element_type=F32))


def _mm_exact_rhs(a, b_bf16):
    hi, mid, lo = _split3(a)
    return (jnp.dot(hi, b_bf16, preferred_element_type=F32)
            + jnp.dot(mid, b_bf16, preferred_element_type=F32)
            + jnp.dot(lo, b_bf16, preferred_element_type=F32))


def _chunk_masks(reverse, width=CHUNK):
    row = lax.broadcasted_iota(jnp.int32, (CHUNK, width), 0)
    col = lax.broadcasted_iota(jnp.int32, (CHUNK, width), 1) % CHUNK
    if reverse:
        strict, incl = row < col, row <= col
    else:
        strict, incl = row > col, row >= col
    same_block = (row // TRI_BLOCK) == (col // TRI_BLOCK)
    return strict, incl, same_block


def _cumsum_mats(reverse):
    row = lax.broadcasted_iota(jnp.int32, (CHUNK, CHUNK), 0)
    col = lax.broadcasted_iota(jnp.int32, (CHUNK, CHUNK), 1)
    lower, upper = (row >= col).astype(BF16), (row <= col).astype(BF16)
    return (upper, lower) if reverse else (lower, upper)


def _unit_tri_inverse_minus_eye(mats, same_block):
    a_d = [jnp.where(same_block, a, 0.0) for a in mats]
    a_o = [a - d for a, d in zip(mats, a_d)]
    xe = [-d for d in a_d]
    p = [_mm(d, d) for d in a_d]
    for level in range(3):
        xp = [_mm(x, q) for x, q in zip(xe, p)]
        xe = [x + q + r for x, q, r in zip(xe, p, xp)]
        if level < 2:
            p = [_mm(q, q) for q in p]
    n = [o + _mm(x, o) for x, o in zip(xe, a_o)]
    n2 = [_mm(m, m) for m in n]
    n3 = [_mm(m, m2) for m, m2 in zip(n, n2)]
    ye = [m2 - m - m3 for m, m2, m3 in zip(n, n2, n3)]
    yx = [_mm(y, x) for y, x in zip(ye, xe)]
    return [y + x + r for y, x, r in zip(ye, xe, yx)]


def _mm_kernel(x_ref, w_ref, o_ref):
    o_ref[...] = _mm(x_ref[...], w_ref[...]).astype(o_ref.dtype)


def _matmul(x, w, *, tm, tn, out_dtype=F32, name="matmul"):
    m, k = x.shape
    n = w.shape[1]
    assert m % tm == 0 and n % tn == 0, (m, n, tm, tn)
    return pl.pallas_call(
        _mm_kernel,
        grid=(m // tm, n // tn),
        in_specs=[pl.BlockSpec((tm, k), lambda i, j: (i, 0)),
                  pl.BlockSpec((k, tn), lambda i, j: (0, j))],
        out_specs=pl.BlockSpec((tm, tn), lambda i, j: (i, j)),
        out_shape=jax.ShapeDtypeStruct((m, n), out_dtype),
        compiler_params=pltpu.CompilerParams(
            dimension_semantics=("parallel", "arbitrary"), vmem_limit_bytes=VMEM_LIMIT),
        name=name,
    )(x, w)


def _swiglu_kernel(x_ref, wg_ref, wu_ref, o_ref):
    x = x_ref[...]
    g = _mm(x, wg_ref[...])
    u = _mm(x, wu_ref[...])
    o_ref[...] = (g * jax.nn.sigmoid(g) * u).astype(o_ref.dtype)


def _swiglu_matmul(x, wg, wu, *, tm, tn):
    m, k = x.shape
    n = wg.shape[1]
    assert m % tm == 0 and n % tn == 0
    return pl.pallas_call(
        _swiglu_kernel,
        grid=(m // tm, n // tn),
        in_specs=[pl.BlockSpec((tm, k), lambda i, j: (i, 0)),
                  pl.BlockSpec((k, tn), lambda i, j: (0, j)),
                  pl.BlockSpec((k, tn), lambda i, j: (0, j))],
        out_specs=pl.BlockSpec((tm, tn), lambda i, j: (i, j)),
        out_shape=jax.ShapeDtypeStruct((m, n), BF16),
        compiler_params=pltpu.CompilerParams(
            dimension_semantics=("parallel", "arbitrary"), vmem_limit_bytes=VMEM_LIMIT),
        name="ffn_in_swiglu",
    )(x, wg, wu)


def _chunk_order(n, n_ctx_chunks, n_chunks, reverse):
    if not reverse:
        return n
    return jnp.where(n < n_ctx_chunks, n_ctx_chunks - 1 - n, n_chunks + n_ctx_chunks - 1 - n)


def _dn_kernel(q_ref, k_ref, v_ref, gcol_ref, grow_ref, bcol_ref, o_ref, s_ref, *, reverse):
    @pl.when(pl.program_id(1) == 0)
    def _():
        s_ref[...] = jnp.zeros_like(s_ref)

    strict, incl, same_block = _chunk_masks(reverse)
    tri, tri_t = _cumsum_mats(reverse)
    gcol = gcol_ref[0, 0]
    grow = grow_ref[0, 0]
    beta_all = bcol_ref[0, 0]
    gcum_col = _mm_exact_lhs(tri, gcol)
    gcum_row = _mm_exact_rhs(grow, tri_t)
    gtot_col = jnp.sum(gcol, axis=0, keepdims=True)

    heads = range(DN_HEADS)
    sls = [slice(h * DN_HEAD_DIM, (h + 1) * DN_HEAD_DIM) for h in heads]
    qs = [q_ref[0, :, sl] for sl in sls]
    ks = [k_ref[0, :, sl] for sl in sls]
    vs = [v_ref[0, :, sl] for sl in sls]
    gcs = [gcum_col[:, h:h + 1] for h in heads]
    gts = [gtot_col[:, h:h + 1] for h in heads]
    betas = [beta_all[:, h:h + 1] for h in heads]
    decs = [jnp.where(incl, jnp.exp(jnp.where(incl, gcs[h] - gcum_row[h:h + 1, :], 0.0)), 0.0)
            for h in heads]
    kbs = [ks[h] * betas[h] for h in heads]
    a_mats = [jnp.where(strict, _mm_nt(kbs[h], ks[h]) * decs[h], 0.0) for h in heads]
    qks = [jnp.where(incl, _mm_nt(qs[h], ks[h]) * decs[h], 0.0) for h in heads]
    t_corr = _unit_tri_inverse_minus_eye(a_mats, same_block)
    egs = [jnp.exp(gc) for gc in gcs]
    rhs = [jnp.concatenate([vs[h] * betas[h], kbs[h] * egs[h]], axis=1) for h in heads]
    sols = [rhs[h] + _mm(t_corr[h], rhs[h]) for h in heads]
    states = [s_ref[h] for h in heads]
    v_new = [sols[h][:, :DN_HEAD_DIM] - _mm(sols[h][:, DN_HEAD_DIM:], states[h]) for h in heads]
    for h in heads:
        o_ref[0, :, sls[h]] = _mm(qs[h] * egs[h], states[h]) + _mm(qks[h], v_new[h])
    for h in heads:
        k_dec = ks[h] * jnp.exp(gts[h] - gcs[h])
        s_ref[h] = states[h] * jnp.exp(gts[h]) + _mm_tn(k_dec, v_new[h])


def _dn_scan(q, k, v, gcol, grow, bcol, *, n_ctx_chunks, reverse):
    b, t, _ = q.shape
    n_chunks = t // CHUNK
    order = functools.partial(_chunk_order, n_ctx_chunks=n_ctx_chunks, n_chunks=n_chunks,
                              reverse=reverse)
    tok_spec = pl.BlockSpec((1, CHUNK, DN_WIDTH), lambda i, n: (i, order(n), 0))
    col_spec = pl.BlockSpec((1, 1, CHUNK, DN_HEADS), lambda i, n: (i, order(n), 0, 0))
    row_spec = pl.BlockSpec((1, 1, DN_HEADS, CHUNK), lambda i, n: (i, order(n), 0, 0))
    return pl.pallas_call(
        functools.partial(_dn_kernel, reverse=reverse),
        grid=(b, n_chunks),
        in_specs=[tok_spec, tok_spec, tok_spec, col_spec, row_spec, col_spec],
        out_specs=tok_spec,
        out_shape=jax.ShapeDtypeStruct((b, t, DN_WIDTH), F32),
        scratch_shapes=[pltpu.VMEM((DN_HEADS, DN_HEAD_DIM, DN_HEAD_DIM), F32)],
        compiler_params=pltpu.CompilerParams(
            dimension_semantics=("parallel", "arbitrary"), vmem_limit_bytes=VMEM_LIMIT),
        name="dn_scan_rev" if reverse else "dn_scan_fwd",
    )(q, k, v, gcol, grow, bcol)


def _rw_kernel(r_ref, lw_ref, k_ref, v_ref, a_ref, b_ref, y_ref, s_ref, *, reverse):
    @pl.when(pl.program_id(1) == 0)
    def _():
        s_ref[...] = jnp.zeros_like(s_ref)

    strict, _, same_block = _chunk_masks(reverse)
    _, incl2, _ = _chunk_masks(reverse, 2 * CHUNK)
    tri, _ = _cumsum_mats(reverse)
    lw = lw_ref[0]
    cum = _mm_exact_lhs(tri, lw)
    tot = jnp.sum(lw, axis=0, keepdims=True)
    w_cum = jnp.exp(cum)
    w_inv = jnp.exp(-cum)
    w_prev = jnp.exp(cum - lw)
    w_rest = jnp.exp(tot - cum)
    w_tot = jnp.exp(tot)
    a_t = a_ref[0] * w_prev
    r_t = r_ref[0] * w_cum
    b_all = b_ref[0]
    k_all = k_ref[0]
    b_t = b_all * w_inv
    k_t = k_all * w_inv
    b_e = b_all * w_rest
    k_e = k_all * w_rest

    heads = range(RW_HEADS)
    sls = [slice(h * RW_HEAD_DIM, (h + 1) * RW_HEAD_DIM) for h in heads]
    ahs = [a_t[:, sl] for sl in sls]
    rhs_ = [r_t[:, sl] for sl in sls]
    vhs = [v_ref[0, :, sl] for sl in sls]
    gs = [_mm_nt(jnp.concatenate([ahs[h], rhs_[h]], axis=0),
                 jnp.concatenate([b_t[:, sls[h]], k_t[:, sls[h]]], axis=0)) for h in heads]
    neg_ab = [jnp.where(strict, -g[:CHUNK, :CHUNK], 0.0) for g in gs]
    a_ak = [jnp.where(strict, g[:CHUNK, CHUNK:], 0.0) for g in gs]
    m_r = [jnp.where(incl2, g[CHUNK:, :], 0.0) for g in gs]
    t_corr = _unit_tri_inverse_minus_eye(neg_ab, same_block)
    rhs = [jnp.concatenate([ahs[h], _mm(a_ak[h], vhs[h])], axis=1) for h in heads]
    sols = [rhs[h] + _mm(t_corr[h], rhs[h]) for h in heads]
    states = [s_ref[h] for h in heads]
    lhs = [_mm_nt(jnp.concatenate([sols[h][:, :RW_HEAD_DIM], rhs_[h]], axis=0), states[h])
           for h in heads]
    uvs = [jnp.concatenate([lhs[h][:CHUNK] + sols[h][:, RW_HEAD_DIM:], vhs[h]], axis=0) for h in heads]
    for h in heads:
        y_ref[0, :, sls[h]] = lhs[h][CHUNK:] + _mm(m_r[h], uvs[h])
    for h in heads:
        bk = jnp.concatenate([b_e[:, sls[h]], k_e[:, sls[h]]], axis=0)
        s_ref[h] = states[h] * w_tot[:, sls[h]] + _mm_tn(uvs[h], bk)


def _rw_scan(r, lw, k, v, a, bvec, *, n_ctx_chunks, reverse):
    b, t, _ = r.shape
    n_chunks = t // CHUNK
    order = functools.partial(_chunk_order, n_ctx_chunks=n_ctx_chunks, n_chunks=n_chunks,
                              reverse=reverse)
    tok_spec = pl.BlockSpec((1, CHUNK, RW_WIDTH), lambda i, n: (i, order(n), 0))
    return pl.pallas_call(
        functools.partial(_rw_kernel, reverse=reverse),
        grid=(b, n_chunks),
        in_specs=[tok_spec] * 6,
        out_specs=tok_spec,
        out_shape=jax.ShapeDtypeStruct((b, t, RW_WIDTH), F32),
        scratch_shapes=[pltpu.VMEM((RW_HEADS, RW_HEAD_DIM, RW_HEAD_DIM), F32)],
        compiler_params=pltpu.CompilerParams(
            dimension_semantics=("parallel", "arbitrary"), vmem_limit_bytes=VMEM_LIMIT),
        name="rw_scan_rev" if reverse else "rw_scan_fwd",
    )(r, lw, k, v, a, bvec)


def _rms(x, gain):
    return x * lax.rsqrt(jnp.mean(x * x, axis=-1, keepdims=True) + NORM_EPS) * gain


def _to_col_major(a, rows):
    b, t, ch = a.shape
    return a.reshape(b, rows, GRID_W, ch).transpose(0, 2, 1, 3).reshape(b, t, ch)


def _to_row_major(a, rows):
    b, t, ch = a.shape
    return a.reshape(b, GRID_W, rows, ch).transpose(0, 2, 1, 3).reshape(b, t, ch)


def _centred_conv(a, w):
    t = a.shape[1]
    half = CONV_K // 2
    ap = jnp.pad(a, ((0, 0), (half, half), (0, 0)))
    out = ap[:, 0:t] * w[0]
    for j in range(1, CONV_K):
        out = out + ap[:, j:j + t] * w[j]
    return out


def _quad_shift(p, rows):
    b, t, ch = p.shape
    g = p.reshape(b, rows, GRID_W, ch)
    left = jnp.pad(g, ((0, 0), (0, 0), (1, 0), (0, 0)))[:, :, :-1]
    right = jnp.pad(g, ((0, 0), (0, 0), (0, 1), (0, 0)))[:, :, 1:]
    up = jnp.pad(g, ((0, 0), (1, 0), (0, 0), (0, 0)))[:, :-1]
    down = jnp.pad(g, ((0, 0), (0, 1), (0, 0), (0, 0)))[:, 1:]
    sel = jnp.arange(ch) % 4
    out = jnp.where(sel == 0, left, jnp.where(sel == 1, right, jnp.where(sel == 2, up, down)))
    return out.reshape(b, t, ch)


def _bi_shift(p):
    prev = jnp.pad(p, ((0, 0), (1, 0), (0, 0)))[:, :-1]
    nxt = jnp.pad(p, ((0, 0), (0, 1), (0, 0)))[:, 1:]
    return jnp.where(jnp.arange(p.shape[-1]) % 2 == 0, prev, nxt)


def _l2n(x):
    return x * lax.rsqrt(jnp.sum(x * x, axis=-1, keepdims=True) + NORM_EPS)


def _permute_w_in(w):
    big_dn = w[:, :4 * DN_WIDTH]
    big_rw = w[:, P_DN:P_DN + 3 * RW_WIDTH]
    small = jnp.concatenate([w[:, 4 * DN_WIDTH:P_DN], w[:, P_DN + 3 * RW_WIDTH:]], axis=1)
    small = jnp.pad(small, ((0, 0), (0, SMALL_W - small.shape[1])))
    return jnp.concatenate([big_dn, big_rw, small], axis=1)


def _lora_weight(w_up, a_up, g_up):
    z = jnp.zeros
    rows = [
        jnp.concatenate([w_up[0], z((W_LORA, 4 * RW_WIDTH), F32)], axis=1),
        jnp.concatenate([z((W_LORA, RW_WIDTH), F32), w_up[1], z((W_LORA, 3 * RW_WIDTH), F32)], axis=1),
        jnp.concatenate([z((A_LORA, 2 * RW_WIDTH), F32), a_up[0], z((A_LORA, 2 * RW_WIDTH), F32)], axis=1),
        jnp.concatenate([z((A_LORA, 3 * RW_WIDTH), F32), a_up[1], z((A_LORA, RW_WIDTH), F32)], axis=1),
        jnp.concatenate([z((G_LORA, 4 * RW_WIDTH), F32), g_up], axis=1),
        z((SMALL_W - 2 * W_LORA - 2 * A_LORA - G_LORA, 5 * RW_WIDTH), F32),
    ]
    return jnp.concatenate(rows, axis=0)


def kernel(x, c, ctx, c_ctx, w_mod, b_mod, mix_pre_g, mix_post_g, ffn_pre_g, ffn_post_g, w_in,
           dn_conv, dn_a_log, dn_dt_bias, dn_out_g, rw_mu, rw_w0, rw_w_up, rw_a0, rw_a_up, rw_g_up,
           rw_k_k, rw_k_a, rw_r_k, rw_ln_w, rw_ln_b, w_out, w_ffn_in, w_ffn_out):
    bsz, seq, d = x.shape
    n_ctx = ctx.shape[1]
    rows = seq // GRID_W
    t_all = n_ctx + seq
    m_all = bsz * t_all
    n_ctx_chunks = n_ctx // CHUNK
    n_chunks = t_all // CHUNK
    tm = 512

    cond = jnp.concatenate([jax.nn.silu(c), jax.nn.silu(c_ctx)[None]], axis=0)
    cond = jnp.pad(cond, ((0, 8 - cond.shape[0]), (0, 0)))

    h = jnp.concatenate([ctx, x], axis=1)

    def per_row(mod_l, idx):
        lat = mod_l[:bsz, idx][:, None, :]
        cx = jnp.broadcast_to(mod_l[bsz, idx][None, None, :], (bsz, 1, d))
        return jnp.concatenate([jnp.broadcast_to(cx, (bsz, n_ctx, d)),
                                jnp.broadcast_to(lat, (bsz, seq, d))], axis=1)

    for l in range(DEPTH):
        mod_l = _matmul(cond, w_mod[l], tm=8, tn=1536, name="adaln") + b_mod[l]
        mod_l = mod_l.reshape(8, 6, d)

        u = _rms(h, mix_pre_g[l]) * (1.0 + per_row(mod_l, 1)) + per_row(mod_l, 0)
        w_in_l = _permute_w_in(w_in[l]).astype(BF16)
        p = _matmul(u.reshape(m_all, d).astype(BF16), w_in_l, tm=tm, tn=768, name="in_proj")
        p = p.reshape(bsz, t_all, -1)
        p_dn_big = p[..., :4 * DN_WIDTH]
        p_rw_big = p[..., 4 * DN_WIDTH:4 * DN_WIDTH + 3 * RW_WIDTH]
        p_small = p[..., 4 * DN_WIDTH + 3 * RW_WIDTH:]
        dn_gates = p_small[..., :4 * DN_HEADS]
        rw_small = p_small[..., 4 * DN_HEADS:4 * DN_HEADS + P_RW - 3 * RW_WIDTH]

        dn_in = jnp.concatenate([p_dn_big, dn_gates], axis=-1)
        dn_in = jnp.concatenate([dn_in[:, :n_ctx], _to_col_major(dn_in[:, n_ctx:], rows)], axis=1)
        conv_w = dn_conv[l]
        qkv = jnp.concatenate([_centred_conv(dn_in[:, :n_ctx, :3 * DN_WIDTH], conv_w),
                               _centred_conv(dn_in[:, n_ctx:, :3 * DN_WIDTH], conv_w)], axis=1)
        qkv = jax.nn.silu(qkv)
        q, k, v = jnp.split(qkv, 3, axis=-1)
        hd = lambda a: a.reshape(bsz, t_all, DN_HEADS, DN_HEAD_DIM)
        q = (_l2n(hd(q)) * DN_HEAD_DIM ** -0.5).reshape(bsz, t_all, DN_WIDTH)
        k = _l2n(hd(k)).reshape(bsz, t_all, DN_WIDTH)
        z = dn_in[..., 3 * DN_WIDTH:4 * DN_WIDTH]
        gates = dn_in[..., 4 * DN_WIDTH:].reshape(bsz, t_all, 4, DN_HEADS)
        beta = jax.nn.sigmoid(gates[:, :, :2])
        g = -jnp.exp(dn_a_log[l])[None, None] * jax.nn.softplus(gates[:, :, 2:] + dn_dt_bias[l][None, None])
        dn_o = 0.0
        for dr in range(2):
            gcol = g[:, :, dr].reshape(bsz, n_chunks, CHUNK, DN_HEADS)
            grow = gcol.transpose(0, 1, 3, 2)
            bcol = beta[:, :, dr].reshape(bsz, n_chunks, CHUNK, DN_HEADS)
            dn_o = dn_o + _dn_scan(q, k, v, gcol, grow, bcol, n_ctx_chunks=n_ctx_chunks,
                                   reverse=(dr == 1))
        dn_o = hd(dn_o)
        dn_o = dn_o * lax.rsqrt(jnp.mean(dn_o * dn_o, axis=-1, keepdims=True) + NORM_EPS) * dn_out_g[l]
        dn_o = (dn_o * jax.nn.silu(hd(z))).reshape(bsz, t_all, DN_WIDTH)
        dn_o = jnp.concatenate([dn_o[:, :n_ctx], _to_row_major(dn_o[:, n_ctx:], rows)], axis=1)

        rw_in = jnp.concatenate([p_rw_big, rw_small], axis=-1)
        shifted = jnp.concatenate([_bi_shift(rw_in[:, :n_ctx]), _quad_shift(rw_in[:, n_ctx:], rows)], axis=1)
        rw_in = rw_in + rw_mu[l] * (shifted - rw_in)
        r = rw_in[..., :RW_WIDTH]
        kr = rw_in[..., RW_WIDTH:2 * RW_WIDTH]
        vr = rw_in[..., 2 * RW_WIDTH:3 * RW_WIDTH]
        off = 3 * RW_WIDTH
        wd = rw_in[..., off:off + 2 * W_LORA]
        ad = rw_in[..., off + 2 * W_LORA:off + 2 * W_LORA + 2 * A_LORA]
        gd = rw_in[..., off + 2 * W_LORA + 2 * A_LORA:]
        lora_in = jnp.concatenate([jnp.tanh(wd), ad, jax.nn.sigmoid(gd)], axis=-1)
        lora_in = jnp.pad(lora_in, ((0, 0), (0, 0), (0, SMALL_W - lora_in.shape[-1])))
        lora_w = _lora_weight(rw_w_up[l], rw_a_up[l], rw_g_up[l]).astype(BF16)
        lora = _matmul(lora_in.reshape(m_all, SMALL_W).astype(BF16), lora_w, tm=tm, tn=1024,
                       name="rw_lora").reshape(bsz, t_all, 5 * RW_WIDTH)
        gate = lora[..., 4 * RW_WIDTH:]
        rh = lambda a: a.reshape(bsz, t_all, RW_HEADS, RW_HEAD_DIM)
        kk = _l2n(rh(kr * rw_k_k[l])).reshape(bsz, t_all, RW_WIDTH)
        rw_y = 0.0
        bonus = 0.0
        for dr in range(2):
            log_w = -jax.nn.softplus(-(rw_w0[l, dr] + lora[..., dr * RW_WIDTH:(dr + 1) * RW_WIDTH])) - 0.5
            log_decay = -jnp.exp(log_w)
            iclr = jax.nn.sigmoid(rw_a0[l, dr] + lora[..., (2 + dr) * RW_WIDTH:(3 + dr) * RW_WIDTH])
            k_d = kr * (1.0 + (iclr - 1.0) * rw_k_a[l])
            rw_y = rw_y + _rw_scan(r, log_decay, k_d, vr, -kk, kk * iclr, n_ctx_chunks=n_ctx_chunks,
                                   reverse=(dr == 1))
            bonus = bonus + jnp.sum(rh(r) * rh(k_d) * rw_r_k[l], axis=-1, keepdims=True) * rh(vr)
        y = rh(rw_y)
        mean = jnp.mean(y, axis=-1, keepdims=True)
        var = jnp.mean(jnp.square(y - mean), axis=-1, keepdims=True)
        yn = ((y - mean) * lax.rsqrt(var + RW_LN_EPS)).reshape(bsz, t_all, RW_WIDTH)
        rw_o = (yn * rw_ln_w[l] + rw_ln_b[l] + bonus.reshape(bsz, t_all, RW_WIDTH)) * gate

        mix = jnp.concatenate([dn_o, rw_o], axis=-1).reshape(m_all, d).astype(BF16)
        mo = _matmul(mix, w_out[l].astype(BF16), tm=tm, tn=1024, name="out_proj").reshape(bsz, t_all, d)
        h = h + per_row(mod_l, 2) * _rms(mo, mix_post_g[l])

        u = _rms(h, ffn_pre_g[l]) * (1.0 + per_row(mod_l, 4)) + per_row(mod_l, 3)
        w_i = w_ffn_in[l].astype(BF16)
        hid = _swiglu_matmul(u.reshape(m_all, d).astype(BF16), w_i[:, :FFN_HIDDEN], w_i[:, FFN_HIDDEN:],
                             tm=tm, tn=512)
        fo = _matmul(hid, w_ffn_out[l].astype(BF16), tm=tm, tn=512, name="ffn_out").reshape(bsz, t_all, d)
        h = h + per_row(mod_l, 5) * _rms(fo, ffn_post_g[l])

    return h[:, n_ctx:]
```

```python
import functools

import jax
import jax.numpy as jnp
from jax import lax
from jax.experimental import pallas as pl
from jax.experimental.pallas import tpu as pltpu

D_MODEL = 2048
DEPTH = 4
GRID_W = 64
DN_WIDTH = 1024
DN_HEAD_DIM = 128
DN_HEADS = 8
RW_WIDTH = 1024
RW_HEAD_DIM = 64
RW_HEADS = 16
CONV_K = 7
CHUNK = 64
W_LORA = 64
A_LORA = 64
G_LORA = 160
FFN_HIDDEN = 5632
P_DN = 4 * DN_WIDTH + 4 * DN_HEADS
P_RW = 3 * RW_WIDTH + 2 * W_LORA + 2 * A_LORA + G_LORA
NORM_EPS = 1e-6
RW_LN_EPS = 64e-5

QKV_W = 3 * 1024
SMALL_W = 512
GD_OFF = 2 * W_LORA + 2 * A_LORA
DN_GATE_OFF = GD_OFF + G_LORA
TRI_BLOCK = 16
HALO = 8
VMEM_LIMIT = 48 * 1024 * 1024

F32 = jnp.float32
BF16 = jnp.bfloat16


def _mm(a, b):
    return jnp.dot(a.astype(BF16), b.astype(BF16), preferred_element_type=F32)


def _mm_nt(a, b):
    return lax.dot_general(a.astype(BF16), b.astype(BF16), (((1,), (1,)), ((), ())),
                           preferred_element_type=F32)


def _mm_tn(a, b):
    return lax.dot_general(a.astype(BF16), b.astype(BF16), (((0,), (0,)), ((), ())),
                           preferred_element_type=F32)


def _split3(x):
    hi = x.astype(BF16)
    r1 = x - hi.astype(F32)
    mid = r1.astype(BF16)
    lo = (r1 - mid.astype(F32)).astype(BF16)
    return hi, mid, lo


def _mm_exact_lhs(a_bf16, b):
    return sum(jnp.dot(a_bf16, part, preferred_element_type=F32) for part in _split3(b))


def _mm_exact_tn(a, b_bf16):
    return sum(lax.dot_general(part, b_bf16, (((0,), (0,)), ((), ())), preferred_element_type=F32)
               for part in _split3(a))


def _seg_sum(x, seg_ref):
    hi = x.astype(BF16)
    lo = (x - hi.astype(F32)).astype(BF16)
    seg = seg_ref[...]
    return (jnp.dot(hi, seg, preferred_element_type=F32)
            + jnp.dot(lo, seg, preferred_element_type=F32))


def _softplus(x):
    return jnp.maximum(x, 0.0) + jnp.log(1.0 + jnp.exp(-jnp.abs(x)))


def _sigmoid(x):
    return 1.0 / (1.0 + jnp.exp(-x))


def _chunk_masks(reverse, width=CHUNK):
    row = lax.broadcasted_iota(jnp.int32, (CHUNK, width), 0)
    col = lax.broadcasted_iota(jnp.int32, (CHUNK, width), 1) % CHUNK
    if reverse:
        strict, incl = row < col, row <= col
    else:
        strict, incl = row > col, row >= col
    same_block = (row // TRI_BLOCK) == (col // TRI_BLOCK)
    return strict, incl, same_block


def _cumsum_mats(reverse):
    row = lax.broadcasted_iota(jnp.int32, (CHUNK, CHUNK), 0)
    col = lax.broadcasted_iota(jnp.int32, (CHUNK, CHUNK), 1)
    lower, upper = (row >= col).astype(BF16), (row <= col).astype(BF16)
    return (upper, lower) if reverse else (lower, upper)


def _unit_tri_inverse_minus_eye(mats, same_block):
    a_d = [jnp.where(same_block, a, 0.0) for a in mats]
    a_o = [a - d for a, d in zip(mats, a_d)]
    xe = [-d for d in a_d]
    p = [_mm(d, d) for d in a_d]
    for level in range(3):
        xp = [_mm(x, q) for x, q in zip(xe, p)]
        xe = [x + q + r for x, q, r in zip(xe, p, xp)]
        if level < 2:
            p = [_mm(q, q) for q in p]
    n = [o + _mm(x, o) for x, o in zip(xe, a_o)]
    n2 = [_mm(m, m) for m in n]
    n3 = [_mm(m, m2) for m, m2 in zip(n, n2)]
    ye = [m2 - m - m3 for m, m2, m3 in zip(n, n2, n3)]
    yx = [_mm(y, x) for y, x in zip(ye, xe)]
    return [y + x + r for y, x, r in zip(ye, xe, yx)]


def _params(semantics):
    return pltpu.CompilerParams(dimension_semantics=semantics, vmem_limit_bytes=VMEM_LIMIT)


def _chunk_spec(width, col_major, chunk_of):
    if col_major:
        return pl.BlockSpec((None, CHUNK, width), lambda b, n: (b, 0, chunk_of(n)))
    return pl.BlockSpec((None, None, CHUNK, width), lambda b, n: (b, chunk_of(n), 0, 0))


def _by_column(shape, col_major):
    return (shape[0], shape[1], shape[2] * shape[3]) if col_major else tuple(shape)


def _const_spec(shape):
    return pl.BlockSpec(shape, lambda b, n: (0,) * len(shape))


def _mm_kernel(x_ref, w_ref, o_ref):
    o_ref[...] = _mm(x_ref[...], w_ref[...]).astype(o_ref.dtype)


def _matmul(x, w, *, tm, tn, name):
    m, k = x.shape
    n = w.shape[1]
    assert m % tm == 0 and n % tn == 0, (m, n, tm, tn)
    return pl.pallas_call(
        _mm_kernel,
        grid=(m // tm, n // tn),
        in_specs=[pl.BlockSpec((tm, k), lambda i, j: (i, 0)),
                  pl.BlockSpec((k, tn), lambda i, j: (0, j))],
        out_specs=pl.BlockSpec((tm, tn), lambda i, j: (i, j)),
        out_shape=jax.ShapeDtypeStruct((m, n), F32),
        compiler_params=_params(("parallel", "arbitrary")),
        name=name,
    )(x, w)


def _norm_mod_kernel(x_ref, g_ref, sh_ref, sc_ref, *rest, swiglu):
    if swiglu:
        wg_ref, wu_ref, o_ref, u_ref = rest
    else:
        w_ref, o_ref, u_ref = rest

    @pl.when(pl.program_id(1) == 0)
    def _():
        x = x_ref[...]
        y = x * lax.rsqrt(jnp.mean(x * x, axis=-1, keepdims=True) + NORM_EPS) * g_ref[...]
        u_ref[...] = (y * (1.0 + sc_ref[0]) + sh_ref[0]).astype(BF16)

    u = u_ref[...]
    if swiglu:
        g = jnp.dot(u, wg_ref[...], preferred_element_type=F32)
        up = jnp.dot(u, wu_ref[...], preferred_element_type=F32)
        o_ref[...] = (g * _sigmoid(g) * up).astype(o_ref.dtype)
    else:
        o_ref[...] = jnp.dot(u, w_ref[...], preferred_element_type=F32).astype(o_ref.dtype)


def _norm_mod_matmul(x, gain, shift, scale, weights, *, tm, tn, out_dtype, name):
    m, d = x.shape
    n = weights[0].shape[1]
    nb = shift.shape[0]
    tiles_per_batch = m // tm // nb
    assert m % (tm * nb) == 0 and n % tn == 0
    mod_spec = pl.BlockSpec((1, 1, d), lambda i, j: (i // tiles_per_batch, 0, 0))
    w_spec = pl.BlockSpec((d, tn), lambda i, j: (0, j))
    return pl.pallas_call(
        functools.partial(_norm_mod_kernel, swiglu=len(weights) == 2),
        grid=(m // tm, n // tn),
        in_specs=[pl.BlockSpec((tm, d), lambda i, j: (i, 0)),
                  pl.BlockSpec((1, d), lambda i, j: (0, 0)), mod_spec, mod_spec]
                 + [w_spec] * len(weights),
        out_specs=pl.BlockSpec((tm, tn), lambda i, j: (i, j)),
        out_shape=jax.ShapeDtypeStruct((m, n), out_dtype),
        scratch_shapes=[pltpu.VMEM((tm, d), BF16)],
        compiler_params=_params(("parallel", "arbitrary")),
        name=name,
    )(x, gain, shift, scale, *weights)


def _mm_resid_kernel(x_ref, w_ref, h_ref, g_ref, gate_ref, o_ref, acc_ref, *, n_k):
    k = pl.program_id(1)

    @pl.when(k == 0)
    def _():
        acc_ref[...] = jnp.zeros_like(acc_ref)

    acc_ref[...] += jnp.dot(x_ref[...], w_ref[...], preferred_element_type=F32)

    @pl.when(k == n_k - 1)
    def _():
        y = acc_ref[...]
        y = y * lax.rsqrt(jnp.mean(y * y, axis=-1, keepdims=True) + NORM_EPS) * g_ref[...]
        o_ref[...] = h_ref[...] + gate_ref[0] * y


def _matmul_residual(x, w, h, gain, gate, *, tm, tk, name):
    m, kdim = x.shape
    d = w.shape[1]
    nb = gate.shape[0]
    tiles_per_batch = m // tm // nb
    n_k = kdim // tk
    assert m % (tm * nb) == 0 and kdim % tk == 0
    return pl.pallas_call(
        functools.partial(_mm_resid_kernel, n_k=n_k),
        grid=(m // tm, n_k),
        in_specs=[pl.BlockSpec((tm, tk), lambda i, k: (i, k)),
                  pl.BlockSpec((tk, d), lambda i, k: (k, 0)),
                  pl.BlockSpec((tm, d), lambda i, k: (i, 0)),
                  pl.BlockSpec((1, d), lambda i, k: (0, 0)),
                  pl.BlockSpec((1, 1, d), lambda i, k: (i // tiles_per_batch, 0, 0))],
        out_specs=pl.BlockSpec((tm, d), lambda i, k: (i, 0)),
        out_shape=jax.ShapeDtypeStruct((m, d), F32),
        scratch_shapes=[pltpu.VMEM((tm, d), F32)],
        compiler_params=_params(("parallel", "arbitrary")),
        name=name,
    )(x, w, h, gain, gate)


def _dn_prep_kernel(cur_ref, prev_ref, next_ref, w_ref, q_ref, k_ref, v_ref, ext_ref, *, n_chunks):
    n = pl.program_id(1)
    keep_prev = jnp.where(n > 0, 1.0, 0.0)
    keep_next = jnp.where(n < n_chunks - 1, 1.0, 0.0)
    ext_ref[0:HALO, :] = prev_ref[CHUNK - HALO:CHUNK, :] * keep_prev
    ext_ref[HALO:HALO + CHUNK, :] = cur_ref[...]
    ext_ref[HALO + CHUNK:2 * HALO + CHUNK, :] = next_ref[0:HALO, :] * keep_next
    outs = (q_ref, k_ref, v_ref)
    for slab in range(3 * DN_HEADS):
        sl = slice(slab * DN_HEAD_DIM, (slab + 1) * DN_HEAD_DIM)
        acc = None
        for j in range(CONV_K):
            start = HALO - CONV_K // 2 + j
            term = ext_ref[start:start + CHUNK, sl] * w_ref[j:j + 1, sl]
            acc = term if acc is None else acc + term
        y = acc * _sigmoid(acc)
        which, head = divmod(slab, DN_HEADS)
        if which < 2:
            y = y * lax.rsqrt(jnp.sum(y * y, axis=-1, keepdims=True) + NORM_EPS)
        if which == 0:
            y = y * DN_HEAD_DIM ** -0.5
        outs[which][:, head * DN_HEAD_DIM:(head + 1) * DN_HEAD_DIM] = y


def _dn_prep(p, conv_w, *, col_major):
    b, n_chunks = p.shape[:2]
    clamp_prev = lambda n: jnp.maximum(n - 1, 0)
    clamp_next = lambda n: jnp.minimum(n + 1, n_chunks - 1)
    in_spec = lambda f: _chunk_spec(QKV_W, col_major, f)
    out_spec = _chunk_spec(DN_WIDTH, False, lambda n: n)
    p = p.reshape(_by_column(p.shape, col_major))
    out_sds = jax.ShapeDtypeStruct((b, n_chunks, CHUNK, DN_WIDTH), F32)
    return pl.pallas_call(
        functools.partial(_dn_prep_kernel, n_chunks=n_chunks),
        grid=(b, n_chunks),
        in_specs=[in_spec(lambda n: n), in_spec(clamp_prev), in_spec(clamp_next),
                  _const_spec((8, QKV_W))],
        out_specs=[out_spec] * 3,
        out_shape=[out_sds] * 3,
        scratch_shapes=[pltpu.VMEM((CHUNK + 2 * HALO, QKV_W), F32)],
        compiler_params=_params(("parallel", "arbitrary")),
        name="dn_prep_lat" if col_major else "dn_prep_ctx",
    )(p, p, p, conv_w)


def _dn_kernel(q_ref, k_ref, v_ref, small_ref, nega_ref, dtb_ref, s0_ref, o_ref, s_ref, *, reverse):
    @pl.when(pl.program_id(1) == 0)
    def _():
        s_ref[...] = s0_ref[...]

    strict, incl, same_block = _chunk_masks(reverse)
    tri, tri_t = _cumsum_mats(reverse)
    gates = small_ref[:, DN_GATE_OFF:DN_GATE_OFF + 4 * DN_HEADS]
    d = 1 if reverse else 0
    beta_all = _sigmoid(gates[:, d * DN_HEADS:(d + 1) * DN_HEADS])
    gcol = nega_ref[...] * _softplus(gates[:, (2 + d) * DN_HEADS:(3 + d) * DN_HEADS] + dtb_ref[...])
    gcum_col = _mm_exact_lhs(tri, gcol)
    gcum_row = _mm_exact_tn(gcol, tri_t)
    gtot_col = jnp.sum(gcol, axis=0, keepdims=True)

    heads = range(DN_HEADS)
    sls = [slice(h * DN_HEAD_DIM, (h + 1) * DN_HEAD_DIM) for h in heads]
    qs = [q_ref[:, sl] for sl in sls]
    ks = [k_ref[:, sl] for sl in sls]
    vs = [v_ref[:, sl] for sl in sls]
    gcs = [gcum_col[:, h:h + 1] for h in heads]
    gts = [gtot_col[:, h:h + 1] for h in heads]
    betas = [beta_all[:, h:h + 1] for h in heads]
    decs = [jnp.where(incl, jnp.exp(jnp.where(incl, gcs[h] - gcum_row[h:h + 1, :], 0.0)), 0.0)
            for h in heads]
    kbs = [ks[h] * betas[h] for h in heads]
    a_mats = [jnp.where(strict, _mm_nt(kbs[h], ks[h]) * decs[h], 0.0) for h in heads]
    qks = [jnp.where(incl, _mm_nt(qs[h], ks[h]) * decs[h], 0.0) for h in heads]
    t_corr = _unit_tri_inverse_minus_eye(a_mats, same_block)
    egs = [jnp.exp(gc) for gc in gcs]
    rhs = [jnp.concatenate([vs[h] * betas[h], kbs[h] * egs[h]], axis=1) for h in heads]
    sols = [rhs[h] + _mm(t_corr[h], rhs[h]) for h in heads]
    states = [s_ref[h] for h in heads]
    v_new = [sols[h][:, :DN_HEAD_DIM] - _mm(sols[h][:, DN_HEAD_DIM:], states[h]) for h in heads]
    for h in heads:
        o_ref[:, sls[h]] = _mm(qs[h] * egs[h], states[h]) + _mm(qks[h], v_new[h])
    for h in heads:
        k_dec = ks[h] * jnp.exp(gts[h] - gcs[h])
        s_ref[h] = states[h] * jnp.exp(gts[h]) + _mm_tn(k_dec, v_new[h])


def _dn_scan(q, k, v, p_small, neg_a, dt_bias, s0, *, col_major, reverse):
    b, n_chunks = q.shape[:2]
    order = (lambda n: n_chunks - 1 - n) if reverse else (lambda n: n)
    tok_spec = _chunk_spec(DN_WIDTH, False, order)
    state_spec = pl.BlockSpec((None, DN_HEADS, DN_HEAD_DIM, DN_HEAD_DIM), lambda i, n: (i, 0, 0, 0))
    o_shape = (b, n_chunks, CHUNK, DN_WIDTH)
    o, s = pl.pallas_call(
        functools.partial(_dn_kernel, reverse=reverse),
        grid=(b, n_chunks),
        in_specs=[tok_spec, tok_spec, tok_spec, _chunk_spec(SMALL_W, col_major, order),
                  _const_spec((1, DN_HEADS)), _const_spec((1, DN_HEADS)), state_spec],
        out_specs=[_chunk_spec(DN_WIDTH, col_major, order), state_spec],
        out_shape=[jax.ShapeDtypeStruct(_by_column(o_shape, col_major), F32),
                   jax.ShapeDtypeStruct(s0.shape, F32)],
        compiler_params=_params(("parallel", "arbitrary")),
        name=("dn_scan_rev" if reverse else "dn_scan_fwd") + ("_lat" if col_major else "_ctx"),
    )(q, k, v, p_small.reshape(_by_column(p_small.shape, col_major)), neg_a, dt_bias, s0)
    return o.reshape(o_shape), s


def _shifted(cur, prev, nxt, n, n_chunks, lat):
    rows = lax.broadcasted_iota(jnp.int32, cur.shape, 0)
    lane = lax.broadcasted_iota(jnp.int32, cur.shape, 1)
    has_prev = jnp.where(n > 0, 1.0, 0.0)
    has_next = jnp.where(n < n_chunks - 1, 1.0, 0.0)
    back = pltpu.roll(cur, 1, axis=0)
    fwd = pltpu.roll(cur, CHUNK - 1, axis=0)
    if lat:
        left = jnp.where(rows == 0, 0.0, back)
        right = jnp.where(rows == CHUNK - 1, 0.0, fwd)
        sel = lane % 4
        return jnp.where(sel == 0, left, jnp.where(sel == 1, right,
                         jnp.where(sel == 2, prev * has_prev, nxt * has_next)))
    before = jnp.where(rows == 0, prev[CHUNK - 1:CHUNK, :] * has_prev, back)
    after = jnp.where(rows == CHUNK - 1, nxt[0:1, :] * has_next, fwd)
    return jnp.where(lane % 2 == 0, before, after)


def _rw_prep_kernel(big_ref, bigp_ref, bign_ref, sm_ref, smp_ref, smn_ref, mub_ref, mus_ref,
                    wup_ref, aup_ref, gup_ref, w0_ref, a0_ref, kk_ref, ka_ref, rk_ref, seg_ref,
                    r_ref, v_ref, a_ref, gate_ref, bonus_ref, kd0_ref, kd1_ref, b0_ref, b1_ref,
                    lw0_ref, lw1_ref, *, n_chunks, lat):
    n = pl.program_id(1)
    big = big_ref[...]
    big = big + mub_ref[...] * (_shifted(big, bigp_ref[...], bign_ref[...], n, n_chunks, lat) - big)
    sm = sm_ref[...]
    sm = sm + mus_ref[...] * (_shifted(sm, smp_ref[...], smn_ref[...], n, n_chunks, lat) - sm)
    r = big[:, :RW_WIDTH]
    k = big[:, RW_WIDTH:2 * RW_WIDTH]
    v = big[:, 2 * RW_WIDTH:]
    r_ref[...] = r
    v_ref[...] = v
    gate_ref[...] = _mm(_sigmoid(sm[:, GD_OFF:GD_OFF + G_LORA]), gup_ref[...])
    kx = k * kk_ref[...]
    kk = kx * lax.rsqrt(_seg_sum(kx * kx, seg_ref) + NORM_EPS)
    a_ref[...] = -kk
    bonus = None
    for d, (kd_ref, b_ref, lw_ref) in enumerate(((kd0_ref, b0_ref, lw0_ref), (kd1_ref, b1_ref, lw1_ref))):
        wd = jnp.tanh(sm[:, d * W_LORA:(d + 1) * W_LORA])
        ad = sm[:, 2 * W_LORA + d * A_LORA:2 * W_LORA + (d + 1) * A_LORA]
        log_w = -_softplus(-(w0_ref[d:d + 1, :] + _mm(wd, wup_ref[d]))) - 0.5
        lw_ref[...] = -jnp.exp(log_w)
        iclr = _sigmoid(a0_ref[d:d + 1, :] + _mm(ad, aup_ref[d]))
        k_d = k * (1.0 + (iclr - 1.0) * ka_ref[...])
        kd_ref[...] = k_d
        b_ref[...] = kk * iclr
        term = _seg_sum(r * k_d * rk_ref[...], seg_ref)
        bonus = term if bonus is None else bonus + term
    bonus_ref[...] = bonus * v


def _rw_prep(p_rw, p_small, mu_big, mu_small, w_up, a_up, g_up, w0, a0, k_k, k_a, r_k, seg, *, lat):
    b, n_chunks = p_rw.shape[:2]
    clamp_prev = lambda n: jnp.maximum(n - 1, 0)
    clamp_next = lambda n: jnp.minimum(n + 1, n_chunks - 1)
    big = lambda f: _chunk_spec(QKV_W, False, f)
    small = lambda f: _chunk_spec(SMALL_W, False, f)
    same = lambda n: n
    out_spec = _chunk_spec(RW_WIDTH, False, same)
    out_sds = jax.ShapeDtypeStruct((b, n_chunks, CHUNK, RW_WIDTH), F32)
    row = lambda w: _const_spec((1, w))
    return pl.pallas_call(
        functools.partial(_rw_prep_kernel, n_chunks=n_chunks, lat=lat),
        grid=(b, n_chunks),
        in_specs=[big(same), big(clamp_prev), big(clamp_next),
                  small(same), small(clamp_prev), small(clamp_next),
                  row(QKV_W), row(SMALL_W),
                  _const_spec((2, W_LORA, RW_WIDTH)), _const_spec((2, A_LORA, RW_WIDTH)),
                  _const_spec((G_LORA, RW_WIDTH)), _const_spec((2, RW_WIDTH)), _const_spec((2, RW_WIDTH)),
                  row(RW_WIDTH), row(RW_WIDTH), row(RW_WIDTH), _const_spec((RW_WIDTH, RW_WIDTH))],
        out_specs=[out_spec] * 11,
        out_shape=[out_sds] * 11,
        compiler_params=_params(("parallel", "arbitrary")),
        name="rw_prep_lat" if lat else "rw_prep_ctx",
    )(p_rw, p_rw, p_rw, p_small, p_small, p_small, mu_big, mu_small, w_up, a_up, g_up, w0, a0, k_k, k_a, r_k, seg)


def _rw_kernel(r_ref, lw_ref, k_ref, v_ref, a_ref, b_ref, s0_ref, y_ref, s_ref, *, reverse):
    @pl.when(pl.program_id(1) == 0)
    def _():
        s_ref[...] = s0_ref[...]

    strict, _, same_block = _chunk_masks(reverse)
    _, incl2, _ = _chunk_masks(reverse, 2 * CHUNK)
    tri, _ = _cumsum_mats(reverse)
    lw = lw_ref[...]
    cum = _mm_exact_lhs(tri, lw)
    tot = jnp.sum(lw, axis=0, keepdims=True)
    w_cum = jnp.exp(cum)
    w_inv = jnp.exp(-cum)
    w_prev = jnp.exp(cum - lw)
    w_rest = jnp.exp(tot - cum)
    w_tot = jnp.exp(tot)
    a_t = a_ref[...] * w_prev
    r_t = r_ref[...] * w_cum
    b_all = b_ref[...]
    k_all = k_ref[...]
    b_t = b_all * w_inv
    k_t = k_all * w_inv
    b_e = b_all * w_rest
    k_e = k_all * w_rest

    heads = range(RW_HEADS)
    sls = [slice(h * RW_HEAD_DIM, (h + 1) * RW_HEAD_DIM) for h in heads]
    ahs = [a_t[:, sl] for sl in sls]
    rhs_ = [r_t[:, sl] for sl in sls]
    vhs = [v_ref[:, sl] for sl in sls]
    gs = [_mm_nt(jnp.concatenate([ahs[h], rhs_[h]], axis=0),
                 jnp.concatenate([b_t[:, sls[h]], k_t[:, sls[h]]], axis=0)) for h in heads]
    neg_ab = [jnp.where(strict, -g[:CHUNK, :CHUNK], 0.0) for g in gs]
    a_ak = [jnp.where(strict, g[:CHUNK, CHUNK:], 0.0) for g in gs]
    m_r = [jnp.where(incl2, g[CHUNK:, :], 0.0) for g in gs]
    t_corr = _unit_tri_inverse_minus_eye(neg_ab, same_block)
    rhs = [jnp.concatenate([ahs[h], _mm(a_ak[h], vhs[h])], axis=1) for h in heads]
    sols = [rhs[h] + _mm(t_corr[h], rhs[h]) for h in heads]
    states = [s_ref[h] for h in heads]
    lhs = [_mm_nt(jnp.concatenate([sols[h][:, :RW_HEAD_DIM], rhs_[h]], axis=0), states[h])
           for h in heads]
    uvs = [jnp.concatenate([lhs[h][:CHUNK] + sols[h][:, RW_HEAD_DIM:], vhs[h]], axis=0) for h in heads]
    for h in heads:
        y_ref[:, sls[h]] = lhs[h][CHUNK:] + _mm(m_r[h], uvs[h])
    for h in heads:
        bk = jnp.concatenate([b_e[:, sls[h]], k_e[:, sls[h]]], axis=0)
        s_ref[h] = states[h] * w_tot[:, sls[h]] + _mm_tn(uvs[h], bk)


def _rw_scan(r, lw, k, v, a, bvec, s0, *, reverse, tag):
    b, n_chunks = r.shape[:2]
    order = (lambda n: n_chunks - 1 - n) if reverse else (lambda n: n)
    tok_spec = _chunk_spec(RW_WIDTH, False, order)
    state_spec = pl.BlockSpec((None, RW_HEADS, RW_HEAD_DIM, RW_HEAD_DIM), lambda i, n: (i, 0, 0, 0))
    return pl.pallas_call(
        functools.partial(_rw_kernel, reverse=reverse),
        grid=(b, n_chunks),
        in_specs=[tok_spec] * 6 + [state_spec],
        out_specs=[tok_spec, state_spec],
        out_shape=[jax.ShapeDtypeStruct(r.shape, F32), jax.ShapeDtypeStruct(s0.shape, F32)],
        compiler_params=_params(("parallel", "arbitrary")),
        name=("rw_scan_rev_" if reverse else "rw_scan_fwd_") + tag,
    )(r, lw, k, v, a, bvec, s0)


def _mix_post_kernel(of_ref, or_ref, z_ref, yf_ref, yr_ref, bonus_ref, gate_ref, og_ref, lnw_ref,
                     lnb_ref, seg_ref, o_ref):
    for h in range(DN_HEADS):
        sl = slice(h * DN_HEAD_DIM, (h + 1) * DN_HEAD_DIM)
        o = of_ref[:, sl] + or_ref[:, sl]
        o = o * lax.rsqrt(jnp.mean(o * o, axis=-1, keepdims=True) + NORM_EPS) * og_ref[...]
        z = z_ref[:, sl]
        o_ref[:, sl] = (o * (z * _sigmoid(z))).astype(o_ref.dtype)
    y = yf_ref[...] + yr_ref[...]
    dev = y - _seg_sum(y, seg_ref) * (1.0 / RW_HEAD_DIM)
    var = _seg_sum(dev * dev, seg_ref) * (1.0 / RW_HEAD_DIM)
    yn = dev * lax.rsqrt(var + RW_LN_EPS)
    out = (yn * lnw_ref[...] + lnb_ref[...] + bonus_ref[...]) * gate_ref[...]
    o_ref[:, DN_WIDTH:] = out.astype(o_ref.dtype)


def _mix_post(o_f, o_r, z, y_f, y_r, bonus, gate, out_g, ln_w, ln_b, seg):
    b, n_chunks = o_f.shape[:2]
    same = lambda n: n
    tok = _chunk_spec(DN_WIDTH, False, same)
    row = lambda w: _const_spec((1, w))
    return pl.pallas_call(
        _mix_post_kernel,
        grid=(b, n_chunks),
        in_specs=[tok] * 7 + [row(DN_HEAD_DIM), row(RW_WIDTH), row(RW_WIDTH),
                              _const_spec((RW_WIDTH, RW_WIDTH))],
        out_specs=_chunk_spec(D_MODEL, False, same),
        out_shape=jax.ShapeDtypeStruct((b, n_chunks, CHUNK, D_MODEL), BF16),
        compiler_params=_params(("parallel", "arbitrary")),
        name="mix_post",
    )(o_f, o_r, z, y_f, y_r, bonus, gate, out_g, ln_w, ln_b, seg)


def _split_w_in(w):
    dn_qkv, dn_z, dn_gates = w[:, :QKV_W], w[:, QKV_W:4 * DN_WIDTH], w[:, 4 * DN_WIDTH:P_DN]
    rw_rkv, rw_small = w[:, P_DN:P_DN + QKV_W], w[:, P_DN + QKV_W:]
    pad = jnp.zeros((w.shape[0], SMALL_W - DN_GATE_OFF - 4 * DN_HEADS), w.dtype)
    return rw_rkv, dn_qkv, dn_z, jnp.concatenate([rw_small, dn_gates, pad], axis=1)


def kernel(x, c, ctx, c_ctx, w_mod, b_mod, mix_pre_g, mix_post_g, ffn_pre_g, ffn_post_g, w_in,
           dn_conv, dn_a_log, dn_dt_bias, dn_out_g, rw_mu, rw_w0, rw_w_up, rw_a0, rw_a_up, rw_g_up,
           rw_k_k, rw_k_a, rw_r_k, rw_ln_w, rw_ln_b, w_out, w_ffn_in, w_ffn_out):
    bsz, seq, d = x.shape
    n_ctx = ctx.shape[1]
    rows = seq // GRID_W
    assert GRID_W == CHUNK and rows == CHUNK and n_ctx % CHUNK == 0

    cond = jnp.concatenate([jax.nn.silu(c), jax.nn.silu(c_ctx)[None]], axis=0)
    cond = jnp.pad(cond, ((0, 8 - cond.shape[0]), (0, 0)))

    head_ids = jnp.arange(RW_WIDTH) // RW_HEAD_DIM
    seg = (head_ids[:, None] == head_ids[None, :]).astype(BF16)

    streams = [
        dict(h=ctx.reshape(bsz, n_ctx // CHUNK, CHUNK, d), lat=False, tm=256),
        dict(h=x.reshape(bsz, rows, GRID_W, d), lat=True, tm=512),
    ]

    for l in range(DEPTH):
        last = l == DEPTH - 1
        mod_l = _matmul(cond, w_mod[l], tm=8, tn=1536, name="adaln") + b_mod[l]
        mod_l = mod_l.reshape(8, 6, d)
        mods = [mod_l[bsz:bsz + 1, :, None, :], mod_l[:bsz, :, None, :]]

        w_in_groups = [w.astype(BF16) for w in _split_w_in(w_in[l])]
        conv_w = jnp.pad(dn_conv[l], ((0, 8 - CONV_K), (0, 0)))
        mu = rw_mu[l]
        mu_big = mu[None, :QKV_W]
        mu_small = jnp.pad(mu[QKV_W:], (0, SMALL_W - (P_RW - QKV_W)))[None]
        w_up = rw_w_up[l].astype(BF16)
        a_up = rw_a_up[l].astype(BF16)
        g_up = rw_g_up[l].astype(BF16)
        neg_a = -jnp.exp(dn_a_log[l])
        dt_b = dn_dt_bias[l]
        out_g = dn_out_g[l][None]
        w_out_l = w_out[l].astype(BF16)
        w_i = w_ffn_in[l].astype(BF16)
        w_o = w_ffn_out[l].astype(BF16)

        dn_state = [jnp.zeros((bsz, DN_HEADS, DN_HEAD_DIM, DN_HEAD_DIM), F32) for _ in range(2)]
        rw_state = [jnp.zeros((bsz, RW_HEADS, RW_HEAD_DIM, RW_HEAD_DIM), F32) for _ in range(2)]

        for si, st in enumerate(streams):
            h = st["h"]
            lat, tm = st["lat"], st["tm"]
            nb = bsz if lat else 1
            n_chunks = h.shape[1]
            m = bsz * n_chunks * CHUNK
            mod = mods[si]
            tag = "lat" if lat else "ctx"

            p_rw, p_dn, p_z, p_small = [
                _norm_mod_matmul(h.reshape(m, d), mix_pre_g[l][None], mod[:, 0], mod[:, 1], [w],
                                 tm=tm, tn=min(w.shape[1], 1024), out_dtype=F32,
                                 name="in_proj_" + tag).reshape(bsz, n_chunks, CHUNK, w.shape[1])
                for w in w_in_groups]

            q, k, v = _dn_prep(p_dn, conv_w, col_major=lat)
            dn_o = []
            for dr in range(2):
                o, dn_state[dr] = _dn_scan(q, k, v, p_small, neg_a[dr][None], dt_b[dr][None], dn_state[dr],
                                           col_major=lat, reverse=(dr == 1))
                dn_o.append(o)

            (r, vr, a, gate, bonus, kd0, kd1, b0, b1, lw0, lw1) = _rw_prep(
                p_rw, p_small, mu_big, mu_small, w_up, a_up, g_up, rw_w0[l], rw_a0[l], rw_k_k[l][None],
                rw_k_a[l][None], rw_r_k[l].reshape(1, RW_WIDTH), seg, lat=lat)
            rw_y = []
            for dr, (kd, bv, lw) in enumerate(((kd0, b0, lw0), (kd1, b1, lw1))):
                y, rw_state[dr] = _rw_scan(r, lw, kd, vr, a, bv, rw_state[dr], reverse=(dr == 1), tag=tag)
                rw_y.append(y)

            if last and not lat:
                continue

            mix = _mix_post(dn_o[0], dn_o[1], p_z, rw_y[0], rw_y[1], bonus, gate, out_g,
                            rw_ln_w[l][None], rw_ln_b[l][None], seg)
            h2 = _matmul_residual(mix.reshape(m, d), w_out_l, h.reshape(m, d), mix_post_g[l][None],
                                  mod[:, 2], tm=tm, tk=d, name="out_proj_" + tag)

            hid = _norm_mod_matmul(h2, ffn_pre_g[l][None], mod[:, 3], mod[:, 4],
                                   [w_i[:, :FFN_HIDDEN], w_i[:, FFN_HIDDEN:]],
                                   tm=tm, tn=512, out_dtype=BF16, name="ffn_in_" + tag)
            h3 = _matmul_residual(hid, w_o, h2, ffn_post_g[l][None], mod[:, 5], tm=tm,
                                  tk=FFN_HIDDEN // 4, name="ffn_out_" + tag)
            st["h"] = h3.reshape(bsz, n_chunks, CHUNK, d)

    return streams[1]["h"].reshape(bsz, seq, d)
```

```python
import functools

import jax
import jax.numpy as jnp
from jax import lax
from jax.experimental import pallas as pl
from jax.experimental.pallas import tpu as pltpu

D_MODEL = 2048
DEPTH = 4
GRID_W = 64
DN_WIDTH = 1024
DN_HEAD_DIM = 128
DN_HEADS = 8
RW_WIDTH = 1024
RW_HEAD_DIM = 64
RW_HEADS = 16
CONV_K = 7
CHUNK = 64
W_LORA = 64
A_LORA = 64
G_LORA = 160
FFN_HIDDEN = 5632
P_DN = 4 * DN_WIDTH + 4 * DN_HEADS
P_RW = 3 * RW_WIDTH + 2 * W_LORA + 2 * A_LORA + G_LORA
NORM_EPS = 1e-6
RW_LN_EPS = 64e-5

QKV_W = 3 * 1024
SMALL_W = 512
GD_OFF = 2 * W_LORA + 2 * A_LORA
DN_GATE_OFF = GD_OFF + G_LORA
TRI_BLOCK = 16
HALO = 8
SEG_TILE = 256
VMEM_LIMIT = 48 * 1024 * 1024

F32 = jnp.float32
BF16 = jnp.bfloat16


def _mm(a, b):
    return jnp.dot(a.astype(BF16), b.astype(BF16), preferred_element_type=F32)


def _mm_nt(a, b):
    return lax.dot_general(a.astype(BF16), b.astype(BF16), (((1,), (1,)), ((), ())),
                           preferred_element_type=F32)


def _mm_tn(a, b):
    return lax.dot_general(a.astype(BF16), b.astype(BF16), (((0,), (0,)), ((), ())),
                           preferred_element_type=F32)


def _split3(x):
    hi = x.astype(BF16)
    r1 = x - hi.astype(F32)
    mid = r1.astype(BF16)
    lo = (r1 - mid.astype(F32)).astype(BF16)
    return hi, mid, lo


def _mm_exact_lhs(a_bf16, b):
    return sum(jnp.dot(a_bf16, part, preferred_element_type=F32) for part in _split3(b))


def _mm_exact_tn(a, b_bf16):
    return sum(lax.dot_general(part, b_bf16, (((0,), (0,)), ((), ())), preferred_element_type=F32)
               for part in _split3(a))


def _seg_sum(x, seg_ref):
    hi = x.astype(BF16)
    lo = (x - hi.astype(F32)).astype(BF16)
    seg = seg_ref[...]
    groups = []
    for g in range(x.shape[1] // SEG_TILE):
        sl = slice(g * SEG_TILE, (g + 1) * SEG_TILE)
        groups.append(jnp.dot(hi[:, sl], seg, preferred_element_type=F32)
                      + jnp.dot(lo[:, sl], seg, preferred_element_type=F32))
    return jnp.concatenate(groups, axis=1)


def _softplus(x):
    return jnp.maximum(x, 0.0) + jnp.log(1.0 + jnp.exp(-jnp.abs(x)))


def _sigmoid(x):
    return 1.0 / (1.0 + jnp.exp(-x))


def _chunk_masks(reverse, width=CHUNK):
    row = lax.broadcasted_iota(jnp.int32, (CHUNK, width), 0)
    col = lax.broadcasted_iota(jnp.int32, (CHUNK, width), 1) % CHUNK
    if reverse:
        strict, incl = row < col, row <= col
    else:
        strict, incl = row > col, row >= col
    same_block = (row // TRI_BLOCK) == (col // TRI_BLOCK)
    return strict, incl, same_block


def _cumsum_mats(reverse):
    row = lax.broadcasted_iota(jnp.int32, (CHUNK, CHUNK), 0)
    col = lax.broadcasted_iota(jnp.int32, (CHUNK, CHUNK), 1)
    lower, upper = (row >= col).astype(BF16), (row <= col).astype(BF16)
    return (upper, lower) if reverse else (lower, upper)


def _unit_tri_inverse_minus_eye(mats, same_block):
    a_d = [jnp.where(same_block, a, 0.0) for a in mats]
    a_o = [a - d for a, d in zip(mats, a_d)]
    xe = [-d for d in a_d]
    p = [_mm(d, d) for d in a_d]
    for level in range(3):
        xp = [_mm(x, q) for x, q in zip(xe, p)]
        xe = [x + q + r for x, q, r in zip(xe, p, xp)]
        if level < 2:
            p = [_mm(q, q) for q in p]
    n = [o + _mm(x, o) for x, o in zip(xe, a_o)]
    n2 = [_mm(m, m) for m in n]
    n3 = [_mm(m, m2) for m, m2 in zip(n, n2)]
    ye = [m2 - m - m3 for m, m2, m3 in zip(n, n2, n3)]
    yx = [_mm(y, x) for y, x in zip(ye, xe)]
    return [y + x + r for y, x, r in zip(ye, xe, yx)]


def _params(semantics):
    return pltpu.CompilerParams(dimension_semantics=semantics, vmem_limit_bytes=VMEM_LIMIT)


def _chunk_spec(width, col_major, chunk_of):
    if col_major:
        return pl.BlockSpec((None, CHUNK, width), lambda b, n: (b, 0, chunk_of(n)))
    return pl.BlockSpec((None, None, CHUNK, width), lambda b, n: (b, chunk_of(n), 0, 0))


def _by_column(shape, col_major):
    return (shape[0], shape[1], shape[2] * shape[3]) if col_major else tuple(shape)


def _const_spec(shape):
    return pl.BlockSpec(shape, lambda b, n: (0,) * len(shape))


def _mm_kernel(x_ref, w_ref, o_ref):
    o_ref[...] = _mm(x_ref[...], w_ref[...]).astype(o_ref.dtype)


def _matmul(x, w, *, tm, tn, name):
    m, k = x.shape
    n = w.shape[1]
    assert m % tm == 0 and n % tn == 0, (m, n, tm, tn)
    return pl.pallas_call(
        _mm_kernel,
        grid=(m // tm, n // tn),
        in_specs=[pl.BlockSpec((tm, k), lambda i, j: (i, 0)),
                  pl.BlockSpec((k, tn), lambda i, j: (0, j))],
        out_specs=pl.BlockSpec((tm, tn), lambda i, j: (i, j)),
        out_shape=jax.ShapeDtypeStruct((m, n), F32),
        compiler_params=_params(("parallel", "arbitrary")),
        name=name,
    )(x, w)


def _norm_mod_kernel(x_ref, g_ref, sh_ref, sc_ref, *rest, swiglu):
    if swiglu:
        wg_ref, wu_ref, o_ref, u_ref = rest
    else:
        w_ref, o_ref, u_ref = rest

    @pl.when(pl.program_id(1) == 0)
    def _():
        x = x_ref[...]
        y = x * lax.rsqrt(jnp.mean(x * x, axis=-1, keepdims=True) + NORM_EPS) * g_ref[...]
        u_ref[...] = (y * (1.0 + sc_ref[0]) + sh_ref[0]).astype(BF16)

    u = u_ref[...]
    if swiglu:
        g = jnp.dot(u, wg_ref[...], preferred_element_type=F32)
        up = jnp.dot(u, wu_ref[...], preferred_element_type=F32)
        o_ref[...] = (g * _sigmoid(g) * up).astype(o_ref.dtype)
    else:
        o_ref[...] = jnp.dot(u, w_ref[...], preferred_element_type=F32).astype(o_ref.dtype)


def _norm_mod_matmul(x, gain, shift, scale, w, *, swiglu, tm, tn, out_dtype, name):
    m, d = x.shape
    n = w.shape[1] // 2 if swiglu else w.shape[1]
    nb = shift.shape[0]
    tiles_per_batch = m // tm // nb
    n_tiles = n // tn
    assert m % (tm * nb) == 0 and n % tn == 0
    mod_spec = pl.BlockSpec((1, 1, d), lambda i, j: (i // tiles_per_batch, 0, 0))
    w_specs = [pl.BlockSpec((d, tn), lambda i, j: (0, j))]
    if swiglu:
        w_specs.append(pl.BlockSpec((d, tn), lambda i, j: (0, j + n_tiles)))
    return pl.pallas_call(
        functools.partial(_norm_mod_kernel, swiglu=swiglu),
        grid=(m // tm, n_tiles),
        in_specs=[pl.BlockSpec((tm, d), lambda i, j: (i, 0)),
                  pl.BlockSpec((1, d), lambda i, j: (0, 0)), mod_spec, mod_spec] + w_specs,
        out_specs=pl.BlockSpec((tm, tn), lambda i, j: (i, j)),
        out_shape=jax.ShapeDtypeStruct((m, n), out_dtype),
        scratch_shapes=[pltpu.VMEM((tm, d), BF16)],
        compiler_params=_params(("parallel", "arbitrary")),
        name=name,
    )(x, gain, shift, scale, *([w, w] if swiglu else [w]))


def _mm_resid_kernel(x_ref, w_ref, h_ref, g_ref, gate_ref, o_ref, acc_ref, *, n_k):
    k = pl.program_id(1)

    @pl.when(k == 0)
    def _():
        acc_ref[...] = jnp.zeros_like(acc_ref)

    acc_ref[...] += jnp.dot(x_ref[...], w_ref[...], preferred_element_type=F32)

    @pl.when(k == n_k - 1)
    def _():
        y = acc_ref[...]
        y = y * lax.rsqrt(jnp.mean(y * y, axis=-1, keepdims=True) + NORM_EPS) * g_ref[...]
        o_ref[...] = h_ref[...] + gate_ref[0] * y


def _matmul_residual(x, w, h, gain, gate, *, tm, tk, name):
    m, kdim = x.shape
    d = w.shape[1]
    nb = gate.shape[0]
    tiles_per_batch = m // tm // nb
    n_k = kdim // tk
    assert m % (tm * nb) == 0 and kdim % tk == 0
    return pl.pallas_call(
        functools.partial(_mm_resid_kernel, n_k=n_k),
        grid=(m // tm, n_k),
        in_specs=[pl.BlockSpec((tm, tk), lambda i, k: (i, k)),
                  pl.BlockSpec((tk, d), lambda i, k: (k, 0)),
                  pl.BlockSpec((tm, d), lambda i, k: (i, 0)),
                  pl.BlockSpec((1, d), lambda i, k: (0, 0)),
                  pl.BlockSpec((1, 1, d), lambda i, k: (i // tiles_per_batch, 0, 0))],
        out_specs=pl.BlockSpec((tm, d), lambda i, k: (i, 0)),
        out_shape=jax.ShapeDtypeStruct((m, d), F32),
        scratch_shapes=[pltpu.VMEM((tm, d), F32)],
        compiler_params=_params(("parallel", "arbitrary")),
        name=name,
    )(x, w, h, gain, gate)


def _dn_prep_kernel(cur_ref, prev_ref, next_ref, w_ref, q_ref, k_ref, v_ref, ext_ref, *, n_chunks):
    n = pl.program_id(1)
    keep_prev = jnp.where(n > 0, 1.0, 0.0)
    keep_next = jnp.where(n < n_chunks - 1, 1.0, 0.0)
    ext_ref[0:HALO, :] = prev_ref[...] * keep_prev
    ext_ref[HALO:HALO + CHUNK, :] = cur_ref[...]
    ext_ref[HALO + CHUNK:2 * HALO + CHUNK, :] = next_ref[...] * keep_next
    outs = (q_ref, k_ref, v_ref)
    for slab in range(3 * DN_HEADS):
        sl = slice(slab * DN_HEAD_DIM, (slab + 1) * DN_HEAD_DIM)
        acc = None
        for j in range(CONV_K):
            start = HALO - CONV_K // 2 + j
            term = ext_ref[start:start + CHUNK, sl] * w_ref[j:j + 1, sl]
            acc = term if acc is None else acc + term
        y = acc * _sigmoid(acc)
        which, head = divmod(slab, DN_HEADS)
        if which < 2:
            y = y * lax.rsqrt(jnp.sum(y * y, axis=-1, keepdims=True) + NORM_EPS)
        if which == 0:
            y = y * DN_HEAD_DIM ** -0.5
        outs[which][:, head * DN_HEAD_DIM:(head + 1) * DN_HEAD_DIM] = y


def _dn_prep(p, conv_w, *, col_major):
    b, n_chunks = p.shape[:2]
    clamp_prev = lambda n: jnp.maximum(n - 1, 0)
    clamp_next = lambda n: jnp.minimum(n + 1, n_chunks - 1)
    out_spec = _chunk_spec(DN_WIDTH, False, lambda n: n)
    p = p.reshape(_by_column(p.shape, col_major))
    last_halo = CHUNK // HALO - 1

    def halo_spec(chunk_of, row_block):
        if col_major:
            return pl.BlockSpec((None, HALO, QKV_W), lambda i, n: (i, row_block, chunk_of(n)))
        return pl.BlockSpec((None, None, HALO, QKV_W), lambda i, n: (i, chunk_of(n), row_block, 0))

    out_sds = jax.ShapeDtypeStruct((b, n_chunks, CHUNK, DN_WIDTH), F32)
    return pl.pallas_call(
        functools.partial(_dn_prep_kernel, n_chunks=n_chunks),
        grid=(b, n_chunks),
        in_specs=[_chunk_spec(QKV_W, col_major, lambda n: n), halo_spec(clamp_prev, last_halo),
                  halo_spec(clamp_next, 0), _const_spec((8, QKV_W))],
        out_specs=[out_spec] * 3,
        out_shape=[out_sds] * 3,
        scratch_shapes=[pltpu.VMEM((CHUNK + 2 * HALO, QKV_W), F32)],
        compiler_params=_params(("parallel", "arbitrary")),
        name="dn_prep_lat" if col_major else "dn_prep_ctx",
    )(p, p, p, conv_w)


def _dn_kernel(q_ref, k_ref, v_ref, small_ref, nega_ref, dtb_ref, s0_ref, o_ref, s_ref, *, reverse):
    @pl.when(pl.program_id(1) == 0)
    def _():
        s_ref[...] = s0_ref[...]

    strict, incl, same_block = _chunk_masks(reverse)
    tri, tri_t = _cumsum_mats(reverse)
    gates = small_ref[:, DN_GATE_OFF:DN_GATE_OFF + 4 * DN_HEADS]
    d = 1 if reverse else 0
    beta_all = _sigmoid(gates[:, d * DN_HEADS:(d + 1) * DN_HEADS])
    gcol = nega_ref[...] * _softplus(gates[:, (2 + d) * DN_HEADS:(3 + d) * DN_HEADS] + dtb_ref[...])
    gcum_col = _mm_exact_lhs(tri, gcol)
    gcum_row = _mm_exact_tn(gcol, tri_t)
    gtot_col = jnp.sum(gcol, axis=0, keepdims=True)

    heads = range(DN_HEADS)
    sls = [slice(h * DN_HEAD_DIM, (h + 1) * DN_HEAD_DIM) for h in heads]
    qs = [q_ref[:, sl] for sl in sls]
    ks = [k_ref[:, sl] for sl in sls]
    vs = [v_ref[:, sl] for sl in sls]
    gcs = [gcum_col[:, h:h + 1] for h in heads]
    gts = [gtot_col[:, h:h + 1] for h in heads]
    betas = [beta_all[:, h:h + 1] for h in heads]
    decs = [jnp.where(incl, jnp.exp(jnp.where(incl, gcs[h] - gcum_row[h:h + 1, :], 0.0)), 0.0)
            for h in heads]
    kbs = [ks[h] * betas[h] for h in heads]
    a_mats = [jnp.where(strict, _mm_nt(kbs[h], ks[h]) * decs[h], 0.0) for h in heads]
    qks = [jnp.where(incl, _mm_nt(qs[h], ks[h]) * decs[h], 0.0) for h in heads]
    t_corr = _unit_tri_inverse_minus_eye(a_mats, same_block)
    egs = [jnp.exp(gc) for gc in gcs]
    rhs = [jnp.concatenate([vs[h] * betas[h], kbs[h] * egs[h]], axis=1) for h in heads]
    sols = [rhs[h] + _mm(t_corr[h], rhs[h]) for h in heads]
    states = [s_ref[h] for h in heads]
    v_new = [sols[h][:, :DN_HEAD_DIM] - _mm(sols[h][:, DN_HEAD_DIM:], states[h]) for h in heads]
    for h in heads:
        o_ref[:, sls[h]] = _mm(qs[h] * egs[h], states[h]) + _mm(qks[h], v_new[h])
    for h in heads:
        k_dec = ks[h] * jnp.exp(gts[h] - gcs[h])
        s_ref[h] = states[h] * jnp.exp(gts[h]) + _mm_tn(k_dec, v_new[h])


def _dn_scan(q, k, v, p_small, neg_a, dt_bias, s0, *, col_major, reverse):
    b, n_chunks = q.shape[:2]
    order = (lambda n: n_chunks - 1 - n) if reverse else (lambda n: n)
    tok_spec = _chunk_spec(DN_WIDTH, False, order)
    state_spec = pl.BlockSpec((None, DN_HEADS, DN_HEAD_DIM, DN_HEAD_DIM), lambda i, n: (i, 0, 0, 0))
    o_shape = (b, n_chunks, CHUNK, DN_WIDTH)
    o, s = pl.pallas_call(
        functools.partial(_dn_kernel, reverse=reverse),
        grid=(b, n_chunks),
        in_specs=[tok_spec, tok_spec, tok_spec, _chunk_spec(SMALL_W, col_major, order),
                  _const_spec((1, DN_HEADS)), _const_spec((1, DN_HEADS)), state_spec],
        out_specs=[_chunk_spec(DN_WIDTH, col_major, order), state_spec],
        out_shape=[jax.ShapeDtypeStruct(_by_column(o_shape, col_major), F32),
                   jax.ShapeDtypeStruct(s0.shape, F32)],
        compiler_params=_params(("parallel", "arbitrary")),
        name=("dn_scan_rev" if reverse else "dn_scan_fwd") + ("_lat" if col_major else "_ctx"),
    )(q, k, v, p_small.reshape(_by_column(p_small.shape, col_major)), neg_a, dt_bias, s0)
    return o.reshape(o_shape), s


def _shifted(cur, prev, nxt, n, n_chunks, lat):
    rows = lax.broadcasted_iota(jnp.int32, cur.shape, 0)
    lane = lax.broadcasted_iota(jnp.int32, cur.shape, 1)
    has_prev = jnp.where(n > 0, 1.0, 0.0)
    has_next = jnp.where(n < n_chunks - 1, 1.0, 0.0)
    back = pltpu.roll(cur, 1, axis=0)
    fwd = pltpu.roll(cur, CHUNK - 1, axis=0)
    if lat:
        left = jnp.where(rows == 0, 0.0, back)
        right = jnp.where(rows == CHUNK - 1, 0.0, fwd)
        sel = lane % 4
        return jnp.where(sel == 0, left, jnp.where(sel == 1, right,
                         jnp.where(sel == 2, prev * has_prev, nxt * has_next)))
    before = jnp.where(rows == 0, prev[CHUNK - 1:CHUNK, :] * has_prev, back)
    after = jnp.where(rows == CHUNK - 1, nxt[0:1, :] * has_next, fwd)
    return jnp.where(lane % 2 == 0, before, after)


def _rw_prep_kernel(big_ref, bigp_ref, bign_ref, sm_ref, smp_ref, smn_ref, mub_ref, mus_ref,
                    wup_ref, aup_ref, gup_ref, w0_ref, a0_ref, kk_ref, ka_ref, rk_ref, seg_ref,
                    r_ref, v_ref, a_ref, gate_ref, bonus_ref, kd0_ref, kd1_ref, b0_ref, b1_ref,
                    lw0_ref, lw1_ref, *, n_chunks, lat):
    n = pl.program_id(1)
    big = big_ref[...]
    big = big + mub_ref[...] * (_shifted(big, bigp_ref[...], bign_ref[...], n, n_chunks, lat) - big)
    sm = sm_ref[...]
    sm = sm + mus_ref[...] * (_shifted(sm, smp_ref[...], smn_ref[...], n, n_chunks, lat) - sm)
    r = big[:, :RW_WIDTH]
    k = big[:, RW_WIDTH:2 * RW_WIDTH]
    v = big[:, 2 * RW_WIDTH:]
    r_ref[...] = r
    v_ref[...] = v
    gate_ref[...] = _mm(_sigmoid(sm[:, GD_OFF:GD_OFF + G_LORA]), gup_ref[...])
    kx = k * kk_ref[...]
    kk = kx * lax.rsqrt(_seg_sum(kx * kx, seg_ref) + NORM_EPS)
    a_ref[...] = -kk
    bonus = None
    for d, (kd_ref, b_ref, lw_ref) in enumerate(((kd0_ref, b0_ref, lw0_ref), (kd1_ref, b1_ref, lw1_ref))):
        wd = jnp.tanh(sm[:, d * W_LORA:(d + 1) * W_LORA])
        ad = sm[:, 2 * W_LORA + d * A_LORA:2 * W_LORA + (d + 1) * A_LORA]
        log_w = -_softplus(-(w0_ref[d:d + 1, :] + _mm(wd, wup_ref[d]))) - 0.5
        lw_ref[...] = -jnp.exp(log_w)
        iclr = _sigmoid(a0_ref[d:d + 1, :] + _mm(ad, aup_ref[d]))
        k_d = k * (1.0 + (iclr - 1.0) * ka_ref[...])
        kd_ref[...] = k_d
        b_ref[...] = kk * iclr
        term = _seg_sum(r * k_d * rk_ref[...], seg_ref)
        bonus = term if bonus is None else bonus + term
    bonus_ref[...] = bonus * v


def _rw_prep(p_rw, p_small, mu_big, mu_small, w_up, a_up, g_up, w0, a0, k_k, k_a, r_k, seg, *, lat):
    b, n_chunks = p_rw.shape[:2]
    clamp_prev = lambda n: jnp.maximum(n - 1, 0)
    clamp_next = lambda n: jnp.minimum(n + 1, n_chunks - 1)
    big = lambda f: _chunk_spec(QKV_W, False, f)
    small = lambda f: _chunk_spec(SMALL_W, False, f)
    same = lambda n: n
    out_spec = _chunk_spec(RW_WIDTH, False, same)
    out_sds = jax.ShapeDtypeStruct((b, n_chunks, CHUNK, RW_WIDTH), F32)
    row = lambda w: _const_spec((1, w))
    return pl.pallas_call(
        functools.partial(_rw_prep_kernel, n_chunks=n_chunks, lat=lat),
        grid=(b, n_chunks),
        in_specs=[big(same), big(clamp_prev), big(clamp_next),
                  small(same), small(clamp_prev), small(clamp_next),
                  row(QKV_W), row(SMALL_W),
                  _const_spec((2, W_LORA, RW_WIDTH)), _const_spec((2, A_LORA, RW_WIDTH)),
                  _const_spec((G_LORA, RW_WIDTH)), _const_spec((2, RW_WIDTH)), _const_spec((2, RW_WIDTH)),
                  row(RW_WIDTH), row(RW_WIDTH), row(RW_WIDTH), _const_spec((SEG_TILE, SEG_TILE))],
        out_specs=[out_spec] * 11,
        out_shape=[out_sds] * 11,
        compiler_params=_params(("parallel", "arbitrary")),
        name="rw_prep_lat" if lat else "rw_prep_ctx",
    )(p_rw, p_rw, p_rw, p_small, p_small, p_small, mu_big, mu_small, w_up, a_up, g_up, w0, a0, k_k, k_a, r_k, seg)


def _rw_kernel(r_ref, lw_ref, k_ref, v_ref, a_ref, b_ref, s0_ref, y_ref, s_ref, *, reverse):
    @pl.when(pl.program_id(1) == 0)
    def _():
        s_ref[...] = s0_ref[...]

    strict, _, same_block = _chunk_masks(reverse)
    _, incl2, _ = _chunk_masks(reverse, 2 * CHUNK)
    tri, _ = _cumsum_mats(reverse)
    lw = lw_ref[...]
    cum = _mm_exact_lhs(tri, lw)
    tot = jnp.sum(lw, axis=0, keepdims=True)
    w_cum = jnp.exp(cum)
    w_inv = jnp.exp(-cum)
    w_prev = jnp.exp(cum - lw)
    w_rest = jnp.exp(tot - cum)
    w_tot = jnp.exp(tot)
    a_t = a_ref[...] * w_prev
    r_t = r_ref[...] * w_cum
    b_all = b_ref[...]
    k_all = k_ref[...]
    b_t = b_all * w_inv
    k_t = k_all * w_inv
    b_e = b_all * w_rest
    k_e = k_all * w_rest

    heads = range(RW_HEADS)
    sls = [slice(h * RW_HEAD_DIM, (h + 1) * RW_HEAD_DIM) for h in heads]
    ahs = [a_t[:, sl] for sl in sls]
    rhs_ = [r_t[:, sl] for sl in sls]
    vhs = [v_ref[:, sl] for sl in sls]
    gs = [_mm_nt(jnp.concatenate([ahs[h], rhs_[h]], axis=0),
                 jnp.concatenate([b_t[:, sls[h]], k_t[:, sls[h]]], axis=0)) for h in heads]
    neg_ab = [jnp.where(strict, -g[:CHUNK, :CHUNK], 0.0) for g in gs]
    a_ak = [jnp.where(strict, g[:CHUNK, CHUNK:], 0.0) for g in gs]
    m_r = [jnp.where(incl2, g[CHUNK:, :], 0.0) for g in gs]
    t_corr = _unit_tri_inverse_minus_eye(neg_ab, same_block)
    rhs = [jnp.concatenate([ahs[h], _mm(a_ak[h], vhs[h])], axis=1) for h in heads]
    sols = [rhs[h] + _mm(t_corr[h], rhs[h]) for h in heads]
    states = [s_ref[h] for h in heads]
    lhs = [_mm_nt(jnp.concatenate([sols[h][:, :RW_HEAD_DIM], rhs_[h]], axis=0), states[h])
           for h in heads]
    uvs = [jnp.concatenate([lhs[h][:CHUNK] + sols[h][:, RW_HEAD_DIM:], vhs[h]], axis=0) for h in heads]
    for h in heads:
        y_ref[:, sls[h]] = lhs[h][CHUNK:] + _mm(m_r[h], uvs[h])
    for h in heads:
        bk = jnp.concatenate([b_e[:, sls[h]], k_e[:, sls[h]]], axis=0)
        s_ref[h] = states[h] * w_tot[:, sls[h]] + _mm_tn(uvs[h], bk)


def _rw_scan(r, lw, k, v, a, bvec, s0, *, reverse, tag):
    b, n_chunks = r.shape[:2]
    order = (lambda n: n_chunks - 1 - n) if reverse else (lambda n: n)
    tok_spec = _chunk_spec(RW_WIDTH, False, order)
    state_spec = pl.BlockSpec((None, RW_HEADS, RW_HEAD_DIM, RW_HEAD_DIM), lambda i, n: (i, 0, 0, 0))
    return pl.pallas_call(
        functools.partial(_rw_kernel, reverse=reverse),
        grid=(b, n_chunks),
        in_specs=[tok_spec] * 6 + [state_spec],
        out_specs=[tok_spec, state_spec],
        out_shape=[jax.ShapeDtypeStruct(r.shape, F32), jax.ShapeDtypeStruct(s0.shape, F32)],
        compiler_params=_params(("parallel", "arbitrary")),
        name=("rw_scan_rev_" if reverse else "rw_scan_fwd_") + tag,
    )(r, lw, k, v, a, bvec, s0)


def _mix_post_kernel(of_ref, or_ref, z_ref, yf_ref, yr_ref, bonus_ref, gate_ref, og_ref, lnw_ref,
                     lnb_ref, seg_ref, o_ref):
    for h in range(DN_HEADS):
        sl = slice(h * DN_HEAD_DIM, (h + 1) * DN_HEAD_DIM)
        o = of_ref[:, sl] + or_ref[:, sl]
        o = o * lax.rsqrt(jnp.mean(o * o, axis=-1, keepdims=True) + NORM_EPS) * og_ref[...]
        z = z_ref[:, sl]
        o_ref[:, sl] = (o * (z * _sigmoid(z))).astype(o_ref.dtype)
    y = yf_ref[...] + yr_ref[...]
    dev = y - _seg_sum(y, seg_ref) * (1.0 / RW_HEAD_DIM)
    var = _seg_sum(dev * dev, seg_ref) * (1.0 / RW_HEAD_DIM)
    yn = dev * lax.rsqrt(var + RW_LN_EPS)
    out = (yn * lnw_ref[...] + lnb_ref[...] + bonus_ref[...]) * gate_ref[...]
    o_ref[:, DN_WIDTH:] = out.astype(o_ref.dtype)


def _mix_post(o_f, o_r, z, y_f, y_r, bonus, gate, out_g, ln_w, ln_b, seg):
    b, n_chunks = o_f.shape[:2]
    same = lambda n: n
    tok = _chunk_spec(DN_WIDTH, False, same)
    row = lambda w: _const_spec((1, w))
    return pl.pallas_call(
        _mix_post_kernel,
        grid=(b, n_chunks),
        in_specs=[tok] * 7 + [row(DN_HEAD_DIM), row(RW_WIDTH), row(RW_WIDTH),
                              _const_spec((SEG_TILE, SEG_TILE))],
        out_specs=_chunk_spec(D_MODEL, False, same),
        out_shape=jax.ShapeDtypeStruct((b, n_chunks, CHUNK, D_MODEL), BF16),
        compiler_params=_params(("parallel", "arbitrary")),
        name="mix_post",
    )(o_f, o_r, z, y_f, y_r, bonus, gate, out_g, ln_w, ln_b, seg)


def _split_w_in(w):
    dn_qkv, dn_z, dn_gates = w[:, :QKV_W], w[:, QKV_W:4 * DN_WIDTH], w[:, 4 * DN_WIDTH:P_DN]
    rw_rkv, rw_small = w[:, P_DN:P_DN + QKV_W], w[:, P_DN + QKV_W:]
    pad = jnp.zeros((w.shape[0], SMALL_W - DN_GATE_OFF - 4 * DN_HEADS), w.dtype)
    return rw_rkv, dn_qkv, dn_z, jnp.concatenate([rw_small, dn_gates, pad], axis=1)


def kernel(x, c, ctx, c_ctx, w_mod, b_mod, mix_pre_g, mix_post_g, ffn_pre_g, ffn_post_g, w_in,
           dn_conv, dn_a_log, dn_dt_bias, dn_out_g, rw_mu, rw_w0, rw_w_up, rw_a0, rw_a_up, rw_g_up,
           rw_k_k, rw_k_a, rw_r_k, rw_ln_w, rw_ln_b, w_out, w_ffn_in, w_ffn_out):
    bsz, seq, d = x.shape
    n_ctx = ctx.shape[1]
    rows = seq // GRID_W
    assert GRID_W == CHUNK and rows == CHUNK and n_ctx % CHUNK == 0

    cond = jnp.concatenate([jax.nn.silu(c), jax.nn.silu(c_ctx)[None]], axis=0)
    cond = jnp.pad(cond, ((0, 8 - cond.shape[0]), (0, 0)))

    head_ids = jnp.arange(SEG_TILE) // RW_HEAD_DIM
    seg = (head_ids[:, None] == head_ids[None, :]).astype(BF16)

    streams = [
        dict(h=ctx.reshape(bsz, n_ctx // CHUNK, CHUNK, d), lat=False, tm=256, tm_pro=256),
        dict(h=x.reshape(bsz, rows, GRID_W, d), lat=True, tm=512, tm_pro=1024),
    ]

    for l in range(DEPTH):
        last = l == DEPTH - 1
        mod_l = _matmul(cond, w_mod[l], tm=8, tn=1536, name="adaln") + b_mod[l]
        mod_l = mod_l.reshape(8, 6, d)
        mods = [mod_l[bsz:bsz + 1, :, None, :], mod_l[:bsz, :, None, :]]

        w_in_groups = [w.astype(BF16) for w in _split_w_in(w_in[l])]
        conv_w = jnp.pad(dn_conv[l], ((0, 8 - CONV_K), (0, 0)))
        mu = rw_mu[l]
        mu_big = mu[None, :QKV_W]
        mu_small = jnp.pad(mu[QKV_W:], (0, SMALL_W - (P_RW - QKV_W)))[None]
        w_up = rw_w_up[l].astype(BF16)
        a_up = rw_a_up[l].astype(BF16)
        g_up = rw_g_up[l].astype(BF16)
        neg_a = -jnp.exp(dn_a_log[l])
        dt_b = dn_dt_bias[l]
        out_g = dn_out_g[l][None]
        w_out_l = w_out[l].astype(BF16)
        w_i = w_ffn_in[l].astype(BF16)
        w_o = w_ffn_out[l].astype(BF16)

        dn_state = [jnp.zeros((bsz, DN_HEADS, DN_HEAD_DIM, DN_HEAD_DIM), F32) for _ in range(2)]
        rw_state = [jnp.zeros((bsz, RW_HEADS, RW_HEAD_DIM, RW_HEAD_DIM), F32) for _ in range(2)]

        for si, st in enumerate(streams):
            h = st["h"]
            lat, tm, tm_pro = st["lat"], st["tm"], st["tm_pro"]
            nb = bsz if lat else 1
            n_chunks = h.shape[1]
            m = bsz * n_chunks * CHUNK
            mod = mods[si]
            tag = "lat" if lat else "ctx"

            p_rw, p_dn, p_z, p_small = [
                _norm_mod_matmul(h.reshape(m, d), mix_pre_g[l][None], mod[:, 0], mod[:, 1], w,
                                 swiglu=False, tm=tm_pro, tn=min(w.shape[1], 1024), out_dtype=F32,
                                 name="in_proj_" + tag).reshape(bsz, n_chunks, CHUNK, w.shape[1])
                for w in w_in_groups]

            q, k, v = _dn_prep(p_dn, conv_w, col_major=lat)
            dn_o = []
            for dr in range(2):
                o, dn_state[dr] = _dn_scan(q, k, v, p_small, neg_a[dr][None], dt_b[dr][None], dn_state[dr],
                                           col_major=lat, reverse=(dr == 1))
                dn_o.append(o)

            (r, vr, a, gate, bonus, kd0, kd1, b0, b1, lw0, lw1) = _rw_prep(
                p_rw, p_small, mu_big, mu_small, w_up, a_up, g_up, rw_w0[l], rw_a0[l], rw_k_k[l][None],
                rw_k_a[l][None], rw_r_k[l].reshape(1, RW_WIDTH), seg, lat=lat)
            rw_y = []
            for dr, (kd, bv, lw) in enumerate(((kd0, b0, lw0), (kd1, b1, lw1))):
                y, rw_state[dr] = _rw_scan(r, lw, kd, vr, a, bv, rw_state[dr], reverse=(dr == 1), tag=tag)
                rw_y.append(y)

            if last and not lat:
                continue

            mix = _mix_post(dn_o[0], dn_o[1], p_z, rw_y[0], rw_y[1], bonus, gate, out_g,
                            rw_ln_w[l][None], rw_ln_b[l][None], seg)
            h2 = _matmul_residual(mix.reshape(m, d), w_out_l, h.reshape(m, d), mix_post_g[l][None],
                                  mod[:, 2], tm=tm, tk=d, name="out_proj_" + tag)

            hid = _norm_mod_matmul(h2, ffn_pre_g[l][None], mod[:, 3], mod[:, 4], w_i, swiglu=True,
                                   tm=tm_pro, tn=512, out_dtype=BF16, name="ffn_in_" + tag)
            h3 = _matmul_residual(hid, w_o, h2, ffn_post_g[l][None], mod[:, 5], tm=tm,
                                  tk=FFN_HIDDEN // 4, name="ffn_out_" + tag)
            st["h"] = h3.reshape(bsz, n_chunks, CHUNK, d)

    return streams[1]["h"].reshape(bsz, seq, d)
```

```python
import functools

import jax
import jax.numpy as jnp
from jax import lax
from jax.experimental import pallas as pl
from jax.experimental.pallas import tpu as pltpu

D_MODEL = 2048
DEPTH = 4
GRID_W = 64
DN_WIDTH = 1024
DN_HEAD_DIM = 128
DN_HEADS = 8
RW_WIDTH = 1024
RW_HEAD_DIM = 64
RW_HEADS = 16
CONV_K = 7
CHUNK = 64
W_LORA = 64
A_LORA = 64
G_LORA = 160
FFN_HIDDEN = 5632
P_DN = 4 * DN_WIDTH + 4 * DN_HEADS
P_RW = 3 * RW_WIDTH + 2 * W_LORA + 2 * A_LORA + G_LORA
NORM_EPS = 1e-6
RW_LN_EPS = 64e-5

QKV_W = 3 * 1024
SMALL_W = 512
GD_OFF = 2 * W_LORA + 2 * A_LORA
DN_GATE_OFF = GD_OFF + G_LORA
TRI_BLOCK = 16
HALO = 16
SEG_TILE = 256
VMEM_LIMIT = 48 * 1024 * 1024

F32 = jnp.float32
BF16 = jnp.bfloat16


def _mm(a, b):
    return jnp.dot(a.astype(BF16), b.astype(BF16), preferred_element_type=F32)


def _mm_nt(a, b):
    return lax.dot_general(a.astype(BF16), b.astype(BF16), (((1,), (1,)), ((), ())),
                           preferred_element_type=F32)


def _mm_tn(a, b):
    return lax.dot_general(a.astype(BF16), b.astype(BF16), (((0,), (0,)), ((), ())),
                           preferred_element_type=F32)


def _split3(x):
    hi = x.astype(BF16)
    r1 = x - hi.astype(F32)
    mid = r1.astype(BF16)
    lo = (r1 - mid.astype(F32)).astype(BF16)
    return hi, mid, lo


def _mm_exact_lhs(a_bf16, b):
    return sum(jnp.dot(a_bf16, part, preferred_element_type=F32) for part in _split3(b))


def _mm_exact_tn(a, b_bf16):
    return sum(lax.dot_general(part, b_bf16, (((0,), (0,)), ((), ())), preferred_element_type=F32)
               for part in _split3(a))


def _seg_sum(x, seg_ref):
    hi = x.astype(BF16)
    lo = (x - hi.astype(F32)).astype(BF16)
    seg = seg_ref[...]
    groups = []
    for g in range(x.shape[1] // SEG_TILE):
        sl = slice(g * SEG_TILE, (g + 1) * SEG_TILE)
        groups.append(jnp.dot(hi[:, sl], seg, preferred_element_type=F32)
                      + jnp.dot(lo[:, sl], seg, preferred_element_type=F32))
    return jnp.concatenate(groups, axis=1)


def _softplus(x):
    return jnp.maximum(x, 0.0) + jnp.log(1.0 + jnp.exp(-jnp.abs(x)))


def _sigmoid(x):
    return 1.0 / (1.0 + jnp.exp(-x))


def _chunk_masks(reverse, width=CHUNK):
    row = lax.broadcasted_iota(jnp.int32, (CHUNK, width), 0)
    col = lax.broadcasted_iota(jnp.int32, (CHUNK, width), 1) % CHUNK
    if reverse:
        strict, incl = row < col, row <= col
    else:
        strict, incl = row > col, row >= col
    same_block = (row // TRI_BLOCK) == (col // TRI_BLOCK)
    return strict, incl, same_block


def _cumsum_mats(reverse):
    row = lax.broadcasted_iota(jnp.int32, (CHUNK, CHUNK), 0)
    col = lax.broadcasted_iota(jnp.int32, (CHUNK, CHUNK), 1)
    lower, upper = (row >= col).astype(BF16), (row <= col).astype(BF16)
    return (upper, lower) if reverse else (lower, upper)


def _unit_tri_inverse_minus_eye(mats, same_block):
    a_d = [jnp.where(same_block, a, 0.0) for a in mats]
    a_o = [a - d for a, d in zip(mats, a_d)]
    xe = [-d for d in a_d]
    p = [_mm(d, d) for d in a_d]
    for level in range(3):
        xp = [_mm(x, q) for x, q in zip(xe, p)]
        xe = [x + q + r for x, q, r in zip(xe, p, xp)]
        if level < 2:
            p = [_mm(q, q) for q in p]
    n = [o + _mm(x, o) for x, o in zip(xe, a_o)]
    n2 = [_mm(m, m) for m in n]
    n3 = [_mm(m, m2) for m, m2 in zip(n, n2)]
    ye = [m2 - m - m3 for m, m2, m3 in zip(n, n2, n3)]
    yx = [_mm(y, x) for y, x in zip(ye, xe)]
    return [y + x + r for y, x, r in zip(ye, xe, yx)]


def _params(semantics):
    return pltpu.CompilerParams(dimension_semantics=semantics, vmem_limit_bytes=VMEM_LIMIT)


def _chunk_spec(width, col_major, chunk_of):
    if col_major:
        return pl.BlockSpec((None, CHUNK, width), lambda b, n: (b, 0, chunk_of(n)))
    return pl.BlockSpec((None, None, CHUNK, width), lambda b, n: (b, chunk_of(n), 0, 0))


def _by_column(shape, col_major):
    return (shape[0], shape[1], shape[2] * shape[3]) if col_major else tuple(shape)


def _const_spec(shape):
    return pl.BlockSpec(shape, lambda b, n: (0,) * len(shape))


def _mm_kernel(x_ref, w_ref, o_ref):
    o_ref[...] = _mm(x_ref[...], w_ref[...]).astype(o_ref.dtype)


def _matmul(x, w, *, tm, tn, name):
    m, k = x.shape
    n = w.shape[1]
    assert m % tm == 0 and n % tn == 0, (m, n, tm, tn)
    return pl.pallas_call(
        _mm_kernel,
        grid=(m // tm, n // tn),
        in_specs=[pl.BlockSpec((tm, k), lambda i, j: (i, 0)),
                  pl.BlockSpec((k, tn), lambda i, j: (0, j))],
        out_specs=pl.BlockSpec((tm, tn), lambda i, j: (i, j)),
        out_shape=jax.ShapeDtypeStruct((m, n), F32),
        compiler_params=_params(("parallel", "arbitrary")),
        name=name,
    )(x, w)


def _norm_mod_kernel(x_ref, g_ref, sh_ref, sc_ref, *rest, swiglu):
    if swiglu:
        wg_ref, wu_ref, o_ref, u_ref = rest
    else:
        w_ref, o_ref, u_ref = rest

    @pl.when(pl.program_id(1) == 0)
    def _():
        x = x_ref[...]
        y = x * lax.rsqrt(jnp.mean(x * x, axis=-1, keepdims=True) + NORM_EPS) * g_ref[...]
        u_ref[...] = (y * (1.0 + sc_ref[0]) + sh_ref[0]).astype(BF16)

    u = u_ref[...]
    if swiglu:
        g = jnp.dot(u, wg_ref[...], preferred_element_type=F32)
        up = jnp.dot(u, wu_ref[...], preferred_element_type=F32)
        o_ref[...] = (g * _sigmoid(g) * up).astype(o_ref.dtype)
    else:
        o_ref[...] = jnp.dot(u, w_ref[...], preferred_element_type=F32).astype(o_ref.dtype)


def _norm_mod_matmul(x, gain, shift, scale, w, *, swiglu, tm, tn, out_dtype, name):
    m, d = x.shape
    n = w.shape[1] // 2 if swiglu else w.shape[1]
    nb = shift.shape[0]
    tiles_per_batch = m // tm // nb
    n_tiles = n // tn
    assert m % (tm * nb) == 0 and n % tn == 0
    mod_spec = pl.BlockSpec((1, 1, d), lambda i, j: (i // tiles_per_batch, 0, 0))
    w_specs = [pl.BlockSpec((d, tn), lambda i, j: (0, j))]
    if swiglu:
        w_specs.append(pl.BlockSpec((d, tn), lambda i, j: (0, j + n_tiles)))
    return pl.pallas_call(
        functools.partial(_norm_mod_kernel, swiglu=swiglu),
        grid=(m // tm, n_tiles),
        in_specs=[pl.BlockSpec((tm, d), lambda i, j: (i, 0)),
                  pl.BlockSpec((1, d), lambda i, j: (0, 0)), mod_spec, mod_spec] + w_specs,
        out_specs=pl.BlockSpec((tm, tn), lambda i, j: (i, j)),
        out_shape=jax.ShapeDtypeStruct((m, n), out_dtype),
        scratch_shapes=[pltpu.VMEM((tm, d), BF16)],
        compiler_params=_params(("parallel", "arbitrary")),
        name=name,
    )(x, gain, shift, scale, *([w, w] if swiglu else [w]))


def _mm_resid_kernel(x_ref, w_ref, h_ref, g_ref, gate_ref, o_ref, acc_ref, *, n_k):
    k = pl.program_id(1)

    @pl.when(k == 0)
    def _():
        acc_ref[...] = jnp.zeros_like(acc_ref)

    acc_ref[...] += jnp.dot(x_ref[...], w_ref[...], preferred_element_type=F32)

    @pl.when(k == n_k - 1)
    def _():
        y = acc_ref[...]
        y = y * lax.rsqrt(jnp.mean(y * y, axis=-1, keepdims=True) + NORM_EPS) * g_ref[...]
        o_ref[...] = h_ref[...] + gate_ref[0] * y


def _matmul_residual(x, w, h, gain, gate, *, tm, tk, name):
    m, kdim = x.shape
    d = w.shape[1]
    nb = gate.shape[0]
    tiles_per_batch = m // tm // nb
    n_k = kdim // tk
    assert m % (tm * nb) == 0 and kdim % tk == 0
    return pl.pallas_call(
        functools.partial(_mm_resid_kernel, n_k=n_k),
        grid=(m // tm, n_k),
        in_specs=[pl.BlockSpec((tm, tk), lambda i, k: (i, k)),
                  pl.BlockSpec((tk, d), lambda i, k: (k, 0)),
                  pl.BlockSpec((tm, d), lambda i, k: (i, 0)),
                  pl.BlockSpec((1, d), lambda i, k: (0, 0)),
                  pl.BlockSpec((1, 1, d), lambda i, k: (i // tiles_per_batch, 0, 0))],
        out_specs=pl.BlockSpec((tm, d), lambda i, k: (i, 0)),
        out_shape=jax.ShapeDtypeStruct((m, d), F32),
        scratch_shapes=[pltpu.VMEM((tm, d), F32)],
        compiler_params=_params(("parallel", "arbitrary")),
        name=name,
    )(x, w, h, gain, gate)


def _dn_prep_kernel(cur_ref, prev_ref, next_ref, w_ref, q_ref, k_ref, v_ref, ext_ref, *, n_chunks):
    n = pl.program_id(1)
    keep_prev = jnp.where(n > 0, 1.0, 0.0)
    keep_next = jnp.where(n < n_chunks - 1, 1.0, 0.0)
    ext_ref[0:HALO, :] = prev_ref[...].astype(F32) * keep_prev
    ext_ref[HALO:HALO + CHUNK, :] = cur_ref[...].astype(F32)
    ext_ref[HALO + CHUNK:2 * HALO + CHUNK, :] = next_ref[...].astype(F32) * keep_next
    outs = (q_ref, k_ref, v_ref)
    for slab in range(3 * DN_HEADS):
        sl = slice(slab * DN_HEAD_DIM, (slab + 1) * DN_HEAD_DIM)
        acc = None
        for j in range(CONV_K):
            start = HALO - CONV_K // 2 + j
            term = ext_ref[start:start + CHUNK, sl] * w_ref[j:j + 1, sl]
            acc = term if acc is None else acc + term
        y = acc * _sigmoid(acc)
        which, head = divmod(slab, DN_HEADS)
        if which < 2:
            y = y * lax.rsqrt(jnp.sum(y * y, axis=-1, keepdims=True) + NORM_EPS)
        if which == 0:
            y = y * DN_HEAD_DIM ** -0.5
        outs[which][:, head * DN_HEAD_DIM:(head + 1) * DN_HEAD_DIM] = y.astype(BF16)


def _dn_prep(p, conv_w, *, col_major):
    b, n_chunks = p.shape[:2]
    clamp_prev = lambda n: jnp.maximum(n - 1, 0)
    clamp_next = lambda n: jnp.minimum(n + 1, n_chunks - 1)
    out_spec = _chunk_spec(DN_WIDTH, False, lambda n: n)
    p = p.reshape(_by_column(p.shape, col_major))
    last_halo = CHUNK // HALO - 1

    def halo_spec(chunk_of, row_block):
        if col_major:
            return pl.BlockSpec((None, HALO, QKV_W), lambda i, n: (i, row_block, chunk_of(n)))
        return pl.BlockSpec((None, None, HALO, QKV_W), lambda i, n: (i, chunk_of(n), row_block, 0))

    out_sds = jax.ShapeDtypeStruct((b, n_chunks, CHUNK, DN_WIDTH), BF16)
    return pl.pallas_call(
        functools.partial(_dn_prep_kernel, n_chunks=n_chunks),
        grid=(b, n_chunks),
        in_specs=[_chunk_spec(QKV_W, col_major, lambda n: n), halo_spec(clamp_prev, last_halo),
                  halo_spec(clamp_next, 0), _const_spec((8, QKV_W))],
        out_specs=[out_spec] * 3,
        out_shape=[out_sds] * 3,
        scratch_shapes=[pltpu.VMEM((CHUNK + 2 * HALO, QKV_W), F32)],
        compiler_params=_params(("parallel", "arbitrary")),
        name="dn_prep_lat" if col_major else "dn_prep_ctx",
    )(p, p, p, conv_w)


def _dn_kernel(q_ref, k_ref, v_ref, small_ref, nega_ref, dtb_ref, s0_ref, o_ref, s_ref, *, reverse):
    @pl.when(pl.program_id(1) == 0)
    def _():
        s_ref[...] = s0_ref[...]

    strict, incl, same_block = _chunk_masks(reverse)
    tri, tri_t = _cumsum_mats(reverse)
    gates = small_ref[:, DN_GATE_OFF:DN_GATE_OFF + 4 * DN_HEADS]
    d = 1 if reverse else 0
    beta_all = _sigmoid(gates[:, d * DN_HEADS:(d + 1) * DN_HEADS])
    gcol = nega_ref[...] * _softplus(gates[:, (2 + d) * DN_HEADS:(3 + d) * DN_HEADS] + dtb_ref[...])
    gcum_col = _mm_exact_lhs(tri, gcol)
    gcum_row = _mm_exact_tn(gcol, tri_t)
    gtot_col = jnp.sum(gcol, axis=0, keepdims=True)

    heads = range(DN_HEADS)
    sls = [slice(h * DN_HEAD_DIM, (h + 1) * DN_HEAD_DIM) for h in heads]
    qs = [q_ref[:, sl].astype(F32) for sl in sls]
    ks = [k_ref[:, sl].astype(F32) for sl in sls]
    vs = [v_ref[:, sl].astype(F32) for sl in sls]
    gcs = [gcum_col[:, h:h + 1] for h in heads]
    gts = [gtot_col[:, h:h + 1] for h in heads]
    betas = [beta_all[:, h:h + 1] for h in heads]
    decs = [jnp.where(incl, jnp.exp(jnp.where(incl, gcs[h] - gcum_row[h:h + 1, :], 0.0)), 0.0)
            for h in heads]
    kbs = [ks[h] * betas[h] for h in heads]
    a_mats = [jnp.where(strict, _mm_nt(kbs[h], ks[h]) * decs[h], 0.0) for h in heads]
    qks = [jnp.where(incl, _mm_nt(qs[h], ks[h]) * decs[h], 0.0) for h in heads]
    t_corr = _unit_tri_inverse_minus_eye(a_mats, same_block)
    egs = [jnp.exp(gc) for gc in gcs]
    rhs = [jnp.concatenate([vs[h] * betas[h], kbs[h] * egs[h]], axis=1) for h in heads]
    sols = [rhs[h] + _mm(t_corr[h], rhs[h]) for h in heads]
    states = [s_ref[h] for h in heads]
    v_new = [sols[h][:, :DN_HEAD_DIM] - _mm(sols[h][:, DN_HEAD_DIM:], states[h]) for h in heads]
    for h in heads:
        o_ref[:, sls[h]] = _mm(qs[h] * egs[h], states[h]) + _mm(qks[h], v_new[h])
    for h in heads:
        k_dec = ks[h] * jnp.exp(gts[h] - gcs[h])
        s_ref[h] = states[h] * jnp.exp(gts[h]) + _mm_tn(k_dec, v_new[h])


def _dn_scan(q, k, v, p_small, neg_a, dt_bias, s0, *, col_major, reverse):
    b, n_chunks = q.shape[:2]
    order = (lambda n: n_chunks - 1 - n) if reverse else (lambda n: n)
    tok_spec = _chunk_spec(DN_WIDTH, False, order)
    state_spec = pl.BlockSpec((None, DN_HEADS, DN_HEAD_DIM, DN_HEAD_DIM), lambda i, n: (i, 0, 0, 0))
    o_shape = (b, n_chunks, CHUNK, DN_WIDTH)
    o, s = pl.pallas_call(
        functools.partial(_dn_kernel, reverse=reverse),
        grid=(b, n_chunks),
        in_specs=[tok_spec, tok_spec, tok_spec, _chunk_spec(SMALL_W, col_major, order),
                  _const_spec((1, DN_HEADS)), _const_spec((1, DN_HEADS)), state_spec],
        out_specs=[_chunk_spec(DN_WIDTH, col_major, order), state_spec],
        out_shape=[jax.ShapeDtypeStruct(_by_column(o_shape, col_major), F32),
                   jax.ShapeDtypeStruct(s0.shape, F32)],
        compiler_params=_params(("parallel", "arbitrary")),
        name=("dn_scan_rev" if reverse else "dn_scan_fwd") + ("_lat" if col_major else "_ctx"),
    )(q, k, v, p_small.reshape(_by_column(p_small.shape, col_major)), neg_a, dt_bias, s0)
    return o.reshape(o_shape), s


def _shifted(cur, prev, nxt, n, n_chunks, lat):
    rows = lax.broadcasted_iota(jnp.int32, cur.shape, 0)
    lane = lax.broadcasted_iota(jnp.int32, cur.shape, 1)
    has_prev = jnp.where(n > 0, 1.0, 0.0)
    has_next = jnp.where(n < n_chunks - 1, 1.0, 0.0)
    back = pltpu.roll(cur, 1, axis=0)
    fwd = pltpu.roll(cur, CHUNK - 1, axis=0)
    if lat:
        left = jnp.where(rows == 0, 0.0, back)
        right = jnp.where(rows == CHUNK - 1, 0.0, fwd)
        sel = lane % 4
        return jnp.where(sel == 0, left, jnp.where(sel == 1, right,
                         jnp.where(sel == 2, prev * has_prev, nxt * has_next)))
    before = jnp.where(rows == 0, prev[CHUNK - 1:CHUNK, :] * has_prev, back)
    after = jnp.where(rows == CHUNK - 1, nxt[0:1, :] * has_next, fwd)
    return jnp.where(lane % 2 == 0, before, after)


def _rw_prep_kernel(big_ref, bigp_ref, bign_ref, sm_ref, smp_ref, smn_ref, mub_ref, mus_ref,
                    wup_ref, aup_ref, gup_ref, w0_ref, a0_ref, kk_ref, ka_ref, rk_ref, seg_ref,
                    r_ref, v_ref, a_ref, gate_ref, bonus_ref, kd0_ref, kd1_ref, b0_ref, b1_ref,
                    lw0_ref, lw1_ref, *, n_chunks, lat):
    n = pl.program_id(1)
    big = big_ref[...].astype(F32)
    big_shift = _shifted(big, bigp_ref[...].astype(F32), bign_ref[...].astype(F32), n, n_chunks, lat)
    big = big + mub_ref[...] * (big_shift - big)
    sm = sm_ref[...]
    sm = sm + mus_ref[...] * (_shifted(sm, smp_ref[...], smn_ref[...], n, n_chunks, lat) - sm)
    r = big[:, :RW_WIDTH]
    k = big[:, RW_WIDTH:2 * RW_WIDTH]
    v = big[:, 2 * RW_WIDTH:]
    r_ref[...] = r.astype(BF16)
    v_ref[...] = v.astype(BF16)
    gate_ref[...] = _mm(_sigmoid(sm[:, GD_OFF:GD_OFF + G_LORA]), gup_ref[...]).astype(BF16)
    kx = k * kk_ref[...]
    kk = kx * lax.rsqrt(_seg_sum(kx * kx, seg_ref) + NORM_EPS)
    a_ref[...] = (-kk).astype(BF16)
    bonus = None
    for d, (kd_ref, b_ref, lw_ref) in enumerate(((kd0_ref, b0_ref, lw0_ref), (kd1_ref, b1_ref, lw1_ref))):
        wd = jnp.tanh(sm[:, d * W_LORA:(d + 1) * W_LORA])
        ad = sm[:, 2 * W_LORA + d * A_LORA:2 * W_LORA + (d + 1) * A_LORA]
        log_w = -_softplus(-(w0_ref[d:d + 1, :] + _mm(wd, wup_ref[d]))) - 0.5
        lw_ref[...] = -jnp.exp(log_w)
        iclr = _sigmoid(a0_ref[d:d + 1, :] + _mm(ad, aup_ref[d]))
        k_d = k * (1.0 + (iclr - 1.0) * ka_ref[...])
        kd_ref[...] = k_d.astype(BF16)
        b_ref[...] = (kk * iclr).astype(BF16)
        term = _seg_sum(r * k_d * rk_ref[...], seg_ref)
        bonus = term if bonus is None else bonus + term
    bonus_ref[...] = (bonus * v).astype(BF16)


def _rw_prep(p_rw, p_small, mu_big, mu_small, w_up, a_up, g_up, w0, a0, k_k, k_a, r_k, seg, *, lat):
    b, n_chunks = p_rw.shape[:2]
    clamp_prev = lambda n: jnp.maximum(n - 1, 0)
    clamp_next = lambda n: jnp.minimum(n + 1, n_chunks - 1)
    big = lambda f: _chunk_spec(QKV_W, False, f)
    small = lambda f: _chunk_spec(SMALL_W, False, f)
    same = lambda n: n
    out_spec = _chunk_spec(RW_WIDTH, False, same)
    out_sds = lambda dt: jax.ShapeDtypeStruct((b, n_chunks, CHUNK, RW_WIDTH), dt)
    row = lambda w: _const_spec((1, w))
    return pl.pallas_call(
        functools.partial(_rw_prep_kernel, n_chunks=n_chunks, lat=lat),
        grid=(b, n_chunks),
        in_specs=[big(same), big(clamp_prev), big(clamp_next),
                  small(same), small(clamp_prev), small(clamp_next),
                  row(QKV_W), row(SMALL_W),
                  _const_spec((2, W_LORA, RW_WIDTH)), _const_spec((2, A_LORA, RW_WIDTH)),
                  _const_spec((G_LORA, RW_WIDTH)), _const_spec((2, RW_WIDTH)), _const_spec((2, RW_WIDTH)),
                  row(RW_WIDTH), row(RW_WIDTH), row(RW_WIDTH), _const_spec((SEG_TILE, SEG_TILE))],
        out_specs=[out_spec] * 11,
        out_shape=[out_sds(BF16)] * 9 + [out_sds(F32)] * 2,
        compiler_params=_params(("parallel", "arbitrary")),
        name="rw_prep_lat" if lat else "rw_prep_ctx",
    )(p_rw, p_rw, p_rw, p_small, p_small, p_small, mu_big, mu_small, w_up, a_up, g_up, w0, a0, k_k, k_a, r_k, seg)


def _rw_kernel(r_ref, lw_ref, k_ref, v_ref, a_ref, b_ref, s0_ref, y_ref, s_ref, *, reverse):
    @pl.when(pl.program_id(1) == 0)
    def _():
        s_ref[...] = s0_ref[...]

    strict, _, same_block = _chunk_masks(reverse)
    _, incl2, _ = _chunk_masks(reverse, 2 * CHUNK)
    tri, _ = _cumsum_mats(reverse)
    lw = lw_ref[...]
    cum = _mm_exact_lhs(tri, lw)
    tot = jnp.sum(lw, axis=0, keepdims=True)
    w_cum = jnp.exp(cum)
    w_inv = jnp.exp(-cum)
    w_prev = jnp.exp(cum - lw)
    w_rest = jnp.exp(tot - cum)
    w_tot = jnp.exp(tot)
    a_t = a_ref[...].astype(F32) * w_prev
    r_t = r_ref[...].astype(F32) * w_cum
    b_all = b_ref[...].astype(F32)
    k_all = k_ref[...].astype(F32)
    v_all = v_ref[...].astype(F32)
    b_t = b_all * w_inv
    k_t = k_all * w_inv
    b_e = b_all * w_rest
    k_e = k_all * w_rest

    heads = range(RW_HEADS)
    sls = [slice(h * RW_HEAD_DIM, (h + 1) * RW_HEAD_DIM) for h in heads]
    ahs = [a_t[:, sl] for sl in sls]
    rhs_ = [r_t[:, sl] for sl in sls]
    vhs = [v_all[:, sl] for sl in sls]
    gs = [_mm_nt(jnp.concatenate([ahs[h], rhs_[h]], axis=0),
                 jnp.concatenate([b_t[:, sls[h]], k_t[:, sls[h]]], axis=0)) for h in heads]
    neg_ab = [jnp.where(strict, -g[:CHUNK, :CHUNK], 0.0) for g in gs]
    a_ak = [jnp.where(strict, g[:CHUNK, CHUNK:], 0.0) for g in gs]
    m_r = [jnp.where(incl2, g[CHUNK:, :], 0.0) for g in gs]
    t_corr = _unit_tri_inverse_minus_eye(neg_ab, same_block)
    rhs = [jnp.concatenate([ahs[h], _mm(a_ak[h], vhs[h])], axis=1) for h in heads]
    sols = [rhs[h] + _mm(t_corr[h], rhs[h]) for h in heads]
    states = [s_ref[h] for h in heads]
    lhs = [_mm_nt(jnp.concatenate([sols[h][:, :RW_HEAD_DIM], rhs_[h]], axis=0), states[h])
           for h in heads]
    uvs = [jnp.concatenate([lhs[h][:CHUNK] + sols[h][:, RW_HEAD_DIM:], vhs[h]], axis=0) for h in heads]
    for h in heads:
        y_ref[:, sls[h]] = lhs[h][CHUNK:] + _mm(m_r[h], uvs[h])
    for h in heads:
        bk = jnp.concatenate([b_e[:, sls[h]], k_e[:, sls[h]]], axis=0)
        s_ref[h] = states[h] * w_tot[:, sls[h]] + _mm_tn(uvs[h], bk)


def _rw_scan(r, lw, k, v, a, bvec, s0, *, reverse, tag):
    b, n_chunks = r.shape[:2]
    order = (lambda n: n_chunks - 1 - n) if reverse else (lambda n: n)
    tok_spec = _chunk_spec(RW_WIDTH, False, order)
    state_spec = pl.BlockSpec((None, RW_HEADS, RW_HEAD_DIM, RW_HEAD_DIM), lambda i, n: (i, 0, 0, 0))
    return pl.pallas_call(
        functools.partial(_rw_kernel, reverse=reverse),
        grid=(b, n_chunks),
        in_specs=[tok_spec] * 6 + [state_spec],
        out_specs=[tok_spec, state_spec],
        out_shape=[jax.ShapeDtypeStruct(r.shape, F32), jax.ShapeDtypeStruct(s0.shape, F32)],
        compiler_params=_params(("parallel", "arbitrary")),
        name=("rw_scan_rev_" if reverse else "rw_scan_fwd_") + tag,
    )(r, lw, k, v, a, bvec, s0)


def _mix_post_kernel(of_ref, or_ref, z_ref, yf_ref, yr_ref, bonus_ref, gate_ref, og_ref, lnw_ref,
                     lnb_ref, seg_ref, o_ref):
    for h in range(DN_HEADS):
        sl = slice(h * DN_HEAD_DIM, (h + 1) * DN_HEAD_DIM)
        o = of_ref[:, sl] + or_ref[:, sl]
        o = o * lax.rsqrt(jnp.mean(o * o, axis=-1, keepdims=True) + NORM_EPS) * og_ref[...]
        z = z_ref[:, sl].astype(F32)
        o_ref[:, sl] = (o * (z * _sigmoid(z))).astype(o_ref.dtype)
    y = yf_ref[...] + yr_ref[...]
    dev = y - _seg_sum(y, seg_ref) * (1.0 / RW_HEAD_DIM)
    var = _seg_sum(dev * dev, seg_ref) * (1.0 / RW_HEAD_DIM)
    yn = dev * lax.rsqrt(var + RW_LN_EPS)
    out = (yn * lnw_ref[...] + lnb_ref[...] + bonus_ref[...].astype(F32)) * gate_ref[...].astype(F32)
    o_ref[:, DN_WIDTH:] = out.astype(o_ref.dtype)


def _mix_post(o_f, o_r, z, y_f, y_r, bonus, gate, out_g, ln_w, ln_b, seg):
    b, n_chunks = o_f.shape[:2]
    same = lambda n: n
    tok = _chunk_spec(DN_WIDTH, False, same)
    row = lambda w: _const_spec((1, w))
    return pl.pallas_call(
        _mix_post_kernel,
        grid=(b, n_chunks),
        in_specs=[tok] * 7 + [row(DN_HEAD_DIM), row(RW_WIDTH), row(RW_WIDTH),
                              _const_spec((SEG_TILE, SEG_TILE))],
        out_specs=_chunk_spec(D_MODEL, False, same),
        out_shape=jax.ShapeDtypeStruct((b, n_chunks, CHUNK, D_MODEL), BF16),
        compiler_params=_params(("parallel", "arbitrary")),
        name="mix_post",
    )(o_f, o_r, z, y_f, y_r, bonus, gate, out_g, ln_w, ln_b, seg)


def _split_w_in(w):
    dn_qkv, dn_z, dn_gates = w[:, :QKV_W], w[:, QKV_W:4 * DN_WIDTH], w[:, 4 * DN_WIDTH:P_DN]
    rw_rkv, rw_small = w[:, P_DN:P_DN + QKV_W], w[:, P_DN + QKV_W:]
    pad = jnp.zeros((w.shape[0], SMALL_W - DN_GATE_OFF - 4 * DN_HEADS), w.dtype)
    return rw_rkv, dn_qkv, dn_z, jnp.concatenate([rw_small, dn_gates, pad], axis=1)


def kernel(x, c, ctx, c_ctx, w_mod, b_mod, mix_pre_g, mix_post_g, ffn_pre_g, ffn_post_g, w_in,
           dn_conv, dn_a_log, dn_dt_bias, dn_out_g, rw_mu, rw_w0, rw_w_up, rw_a0, rw_a_up, rw_g_up,
           rw_k_k, rw_k_a, rw_r_k, rw_ln_w, rw_ln_b, w_out, w_ffn_in, w_ffn_out):
    bsz, seq, d = x.shape
    n_ctx = ctx.shape[1]
    rows = seq // GRID_W
    assert GRID_W == CHUNK and rows == CHUNK and n_ctx % CHUNK == 0

    cond = jnp.concatenate([jax.nn.silu(c), jax.nn.silu(c_ctx)[None]], axis=0)
    cond = jnp.pad(cond, ((0, 8 - cond.shape[0]), (0, 0)))

    head_ids = jnp.arange(SEG_TILE) // RW_HEAD_DIM
    seg = (head_ids[:, None] == head_ids[None, :]).astype(BF16)

    streams = [
        dict(h=ctx.reshape(bsz, n_ctx // CHUNK, CHUNK, d), lat=False, tm=256, tm_pro=256),
        dict(h=x.reshape(bsz, rows, GRID_W, d), lat=True, tm=512, tm_pro=1024),
    ]

    for l in range(DEPTH):
        last = l == DEPTH - 1
        mod_l = _matmul(cond, w_mod[l], tm=8, tn=1536, name="adaln") + b_mod[l]
        mod_l = mod_l.reshape(8, 6, d)
        mods = [mod_l[bsz:bsz + 1, :, None, :], mod_l[:bsz, :, None, :]]

        w_in_groups = [w.astype(BF16) for w in _split_w_in(w_in[l])]
        conv_w = jnp.pad(dn_conv[l], ((0, 8 - CONV_K), (0, 0)))
        mu = rw_mu[l]
        mu_big = mu[None, :QKV_W]
        mu_small = jnp.pad(mu[QKV_W:], (0, SMALL_W - (P_RW - QKV_W)))[None]
        w_up = rw_w_up[l].astype(BF16)
        a_up = rw_a_up[l].astype(BF16)
        g_up = rw_g_up[l].astype(BF16)
        neg_a = -jnp.exp(dn_a_log[l])
        dt_b = dn_dt_bias[l]
        out_g = dn_out_g[l][None]
        w_out_l = w_out[l].astype(BF16)
        w_i = w_ffn_in[l].astype(BF16)
        w_o = w_ffn_out[l].astype(BF16)

        dn_state = [jnp.zeros((bsz, DN_HEADS, DN_HEAD_DIM, DN_HEAD_DIM), F32) for _ in range(2)]
        rw_state = [jnp.zeros((bsz, RW_HEADS, RW_HEAD_DIM, RW_HEAD_DIM), F32) for _ in range(2)]

        for si, st in enumerate(streams):
            h = st["h"]
            lat, tm, tm_pro = st["lat"], st["tm"], st["tm_pro"]
            nb = bsz if lat else 1
            n_chunks = h.shape[1]
            m = bsz * n_chunks * CHUNK
            mod = mods[si]
            tag = "lat" if lat else "ctx"

            p_rw, p_dn, p_z, p_small = [
                _norm_mod_matmul(h.reshape(m, d), mix_pre_g[l][None], mod[:, 0], mod[:, 1], w,
                                 swiglu=False, tm=tm_pro, tn=min(w.shape[1], 1024),
                                 out_dtype=F32 if w.shape[1] == SMALL_W else BF16,
                                 name="in_proj_" + tag).reshape(bsz, n_chunks, CHUNK, w.shape[1])
                for w in w_in_groups]

            q, k, v = _dn_prep(p_dn, conv_w, col_major=lat)
            dn_o = []
            for dr in range(2):
                o, dn_state[dr] = _dn_scan(q, k, v, p_small, neg_a[dr][None], dt_b[dr][None], dn_state[dr],
                                           col_major=lat, reverse=(dr == 1))
                dn_o.append(o)

            (r, vr, a, gate, bonus, kd0, kd1, b0, b1, lw0, lw1) = _rw_prep(
                p_rw, p_small, mu_big, mu_small, w_up, a_up, g_up, rw_w0[l], rw_a0[l], rw_k_k[l][None],
                rw_k_a[l][None], rw_r_k[l].reshape(1, RW_WIDTH), seg, lat=lat)
            rw_y = []
            for dr, (kd, bv, lw) in enumerate(((kd0, b0, lw0), (kd1, b1, lw1))):
                y, rw_state[dr] = _rw_scan(r, lw, kd, vr, a, bv, rw_state[dr], reverse=(dr == 1), tag=tag)
                rw_y.append(y)

            if last and not lat:
                continue

            mix = _mix_post(dn_o[0], dn_o[1], p_z, rw_y[0], rw_y[1], bonus, gate, out_g,
                            rw_ln_w[l][None], rw_ln_b[l][None], seg)
            h2 = _matmul_residual(mix.reshape(m, d), w_out_l, h.reshape(m, d), mix_post_g[l][None],
                                  mod[:, 2], tm=tm, tk=d, name="out_proj_" + tag)

            hid = _norm_mod_matmul(h2, ffn_pre_g[l][None], mod[:, 3], mod[:, 4], w_i, swiglu=True,
                                   tm=tm_pro, tn=512, out_dtype=BF16, name="ffn_in_" + tag)
            h3 = _matmul_residual(hid, w_o, h2, ffn_post_g[l][None], mod[:, 5], tm=tm,
                                  tk=FFN_HIDDEN // 4, name="ffn_out_" + tag)
            st["h"] = h3.reshape(bsz, n_chunks, CHUNK, d)

    return streams[1]["h"].reshape(bsz, seq, d)
```

```python
import functools

import jax
import jax.numpy as jnp
from jax import lax
from jax.experimental import pallas as pl
from jax.experimental.pallas import tpu as pltpu

D_MODEL = 2048
DEPTH = 4
GRID_W = 64
DN_WIDTH = 1024
DN_HEAD_DIM = 128
DN_HEADS = 8
RW_WIDTH = 1024
RW_HEAD_DIM = 64
RW_HEADS = 16
CONV_K = 7
CHUNK = 64
W_LORA = 64
A_LORA = 64
G_LORA = 160
FFN_HIDDEN = 5632
P_DN = 4 * DN_WIDTH + 4 * DN_HEADS
P_RW = 3 * RW_WIDTH + 2 * W_LORA + 2 * A_LORA + G_LORA
NORM_EPS = 1e-6
RW_LN_EPS = 64e-5

QKV_W = 3 * 1024
SMALL_W = 512
GD_OFF = 2 * W_LORA + 2 * A_LORA
DN_GATE_OFF = GD_OFF + G_LORA
TRI_BLOCK = 16
HALO = 16
SEG_TILE = 256
VMEM_LIMIT = 48 * 1024 * 1024

F32 = jnp.float32
BF16 = jnp.bfloat16


def _mm(a, b):
    return jnp.dot(a.astype(BF16), b.astype(BF16), preferred_element_type=F32)


def _mm_nt(a, b):
    return lax.dot_general(a.astype(BF16), b.astype(BF16), (((1,), (1,)), ((), ())),
                           preferred_element_type=F32)


def _mm_tn(a, b):
    return lax.dot_general(a.astype(BF16), b.astype(BF16), (((0,), (0,)), ((), ())),
                           preferred_element_type=F32)


def _split3(x):
    hi = x.astype(BF16)
    r1 = x - hi.astype(F32)
    mid = r1.astype(BF16)
    lo = (r1 - mid.astype(F32)).astype(BF16)
    return hi, mid, lo


def _mm_exact_lhs(a_bf16, b):
    return sum(jnp.dot(a_bf16, part, preferred_element_type=F32) for part in _split3(b))


def _mm_exact_tn(a, b_bf16):
    return sum(lax.dot_general(part, b_bf16, (((0,), (0,)), ((), ())), preferred_element_type=F32)
               for part in _split3(a))


def _seg_sum(x, seg_ref):
    hi = x.astype(BF16)
    lo = (x - hi.astype(F32)).astype(BF16)
    seg = seg_ref[...]
    groups = []
    for g in range(x.shape[1] // SEG_TILE):
        sl = slice(g * SEG_TILE, (g + 1) * SEG_TILE)
        groups.append(jnp.dot(hi[:, sl], seg, preferred_element_type=F32)
                      + jnp.dot(lo[:, sl], seg, preferred_element_type=F32))
    return jnp.concatenate(groups, axis=1)


def _softplus(x):
    return jnp.maximum(x, 0.0) + jnp.log(1.0 + jnp.exp(-jnp.abs(x)))


def _sigmoid(x):
    return 1.0 / (1.0 + jnp.exp(-x))


def _chunk_masks(reverse, width=CHUNK):
    row = lax.broadcasted_iota(jnp.int32, (CHUNK, width), 0)
    col = lax.broadcasted_iota(jnp.int32, (CHUNK, width), 1) % CHUNK
    if reverse:
        strict, incl = row < col, row <= col
    else:
        strict, incl = row > col, row >= col
    same_block = (row // TRI_BLOCK) == (col // TRI_BLOCK)
    return strict, incl, same_block


def _cumsum_mats(reverse):
    row = lax.broadcasted_iota(jnp.int32, (CHUNK, CHUNK), 0)
    col = lax.broadcasted_iota(jnp.int32, (CHUNK, CHUNK), 1)
    lower, upper = (row >= col).astype(BF16), (row <= col).astype(BF16)
    return (upper, lower) if reverse else (lower, upper)


def _unit_tri_inverse_minus_eye(mats, same_block):
    a_d = [jnp.where(same_block, a, 0.0) for a in mats]
    a_o = [a - d for a, d in zip(mats, a_d)]
    xe = [-d for d in a_d]
    p = [_mm(d, d) for d in a_d]
    for level in range(3):
        if level < 2:
            both = [_mm(jnp.concatenate([x, q], axis=0), q) for x, q in zip(xe, p)]
            xp = [r[:CHUNK] for r in both]
            p_next = [r[CHUNK:] for r in both]
        else:
            xp = [_mm(x, q) for x, q in zip(xe, p)]
            p_next = None
        xe = [x + q + r for x, q, r in zip(xe, p, xp)]
        p = p_next
    n = [o + _mm(x, o) for x, o in zip(xe, a_o)]
    n2 = [_mm(m, m) for m in n]
    n3 = [_mm(m, m2) for m, m2 in zip(n, n2)]
    ye = [m2 - m - m3 for m, m2, m3 in zip(n, n2, n3)]
    yx = [_mm(y, x) for y, x in zip(ye, xe)]
    return [y + x + r for y, x, r in zip(ye, xe, yx)]


def _params(semantics):
    return pltpu.CompilerParams(dimension_semantics=semantics, vmem_limit_bytes=VMEM_LIMIT)


def _chunk_spec(width, col_major, chunk_of):
    if col_major:
        return pl.BlockSpec((None, CHUNK, width), lambda b, n: (b, 0, chunk_of(n)))
    return pl.BlockSpec((None, None, CHUNK, width), lambda b, n: (b, chunk_of(n), 0, 0))


def _by_column(shape, col_major):
    return (shape[0], shape[1], shape[2] * shape[3]) if col_major else tuple(shape)


def _scan_spec(batch, width, col_major, chunk_of):
    if col_major:
        return pl.BlockSpec((batch, CHUNK, width), lambda n: (0, 0, chunk_of(n)))
    return pl.BlockSpec((batch, None, CHUNK, width), lambda n: (0, chunk_of(n), 0, 0))


def _const_spec(shape):
    return pl.BlockSpec(shape, lambda *_: (0,) * len(shape))


def _mm_kernel(x_ref, w_ref, o_ref):
    o_ref[...] = _mm(x_ref[...], w_ref[...]).astype(o_ref.dtype)


def _matmul(x, w, *, tm, tn, name):
    m, k = x.shape
    n = w.shape[1]
    assert m % tm == 0 and n % tn == 0, (m, n, tm, tn)
    return pl.pallas_call(
        _mm_kernel,
        grid=(m // tm, n // tn),
        in_specs=[pl.BlockSpec((tm, k), lambda i, j: (i, 0)),
                  pl.BlockSpec((k, tn), lambda i, j: (0, j))],
        out_specs=pl.BlockSpec((tm, tn), lambda i, j: (i, j)),
        out_shape=jax.ShapeDtypeStruct((m, n), F32),
        compiler_params=_params(("parallel", "arbitrary")),
        name=name,
    )(x, w)


def _norm_mod_kernel(x_ref, g_ref, sh_ref, sc_ref, *rest, swiglu):
    if swiglu:
        wg_ref, wu_ref, o_ref, u_ref = rest
    else:
        w_ref, o_ref, u_ref = rest

    @pl.when(pl.program_id(1) == 0)
    def _():
        x = x_ref[...]
        y = x * lax.rsqrt(jnp.mean(x * x, axis=-1, keepdims=True) + NORM_EPS) * g_ref[...]
        u_ref[...] = (y * (1.0 + sc_ref[0]) + sh_ref[0]).astype(BF16)

    u = u_ref[...]
    if swiglu:
        g = jnp.dot(u, wg_ref[...], preferred_element_type=F32)
        up = jnp.dot(u, wu_ref[...], preferred_element_type=F32)
        o_ref[...] = (g * _sigmoid(g) * up).astype(o_ref.dtype)
    else:
        o_ref[...] = jnp.dot(u, w_ref[...], preferred_element_type=F32).astype(o_ref.dtype)


def _norm_mod_matmul(x, gain, shift, scale, w, *, swiglu, tm, tn, out_dtype, name):
    m, d = x.shape
    n = w.shape[1] // 2 if swiglu else w.shape[1]
    nb = shift.shape[0]
    tiles_per_batch = m // tm // nb
    n_tiles = n // tn
    assert m % (tm * nb) == 0 and n % tn == 0
    mod_spec = pl.BlockSpec((1, 1, d), lambda i, j: (i // tiles_per_batch, 0, 0))
    w_specs = [pl.BlockSpec((d, tn), lambda i, j: (0, j))]
    if swiglu:
        w_specs.append(pl.BlockSpec((d, tn), lambda i, j: (0, j + n_tiles)))
    return pl.pallas_call(
        functools.partial(_norm_mod_kernel, swiglu=swiglu),
        grid=(m // tm, n_tiles),
        in_specs=[pl.BlockSpec((tm, d), lambda i, j: (i, 0)),
                  pl.BlockSpec((1, d), lambda i, j: (0, 0)), mod_spec, mod_spec] + w_specs,
        out_specs=pl.BlockSpec((tm, tn), lambda i, j: (i, j)),
        out_shape=jax.ShapeDtypeStruct((m, n), out_dtype),
        scratch_shapes=[pltpu.VMEM((tm, d), BF16)],
        compiler_params=_params(("parallel", "arbitrary")),
        name=name,
    )(x, gain, shift, scale, *([w, w] if swiglu else [w]))


def _mm_resid_kernel(x_ref, w_ref, h_ref, g_ref, gate_ref, o_ref, acc_ref, *, n_k):
    k = pl.program_id(1)

    @pl.when(k == 0)
    def _():
        acc_ref[...] = jnp.zeros_like(acc_ref)

    acc_ref[...] += jnp.dot(x_ref[...], w_ref[...], preferred_element_type=F32)

    @pl.when(k == n_k - 1)
    def _():
        y = acc_ref[...]
        y = y * lax.rsqrt(jnp.mean(y * y, axis=-1, keepdims=True) + NORM_EPS) * g_ref[...]
        o_ref[...] = h_ref[...] + gate_ref[0] * y


def _matmul_residual(x, w, h, gain, gate, *, tm, tk, name):
    m, kdim = x.shape
    d = w.shape[1]
    nb = gate.shape[0]
    tiles_per_batch = m // tm // nb
    n_k = kdim // tk
    assert m % (tm * nb) == 0 and kdim % tk == 0
    return pl.pallas_call(
        functools.partial(_mm_resid_kernel, n_k=n_k),
        grid=(m // tm, n_k),
        in_specs=[pl.BlockSpec((tm, tk), lambda i, k: (i, k)),
                  pl.BlockSpec((tk, d), lambda i, k: (k, 0)),
                  pl.BlockSpec((tm, d), lambda i, k: (i, 0)),
                  pl.BlockSpec((1, d), lambda i, k: (0, 0)),
                  pl.BlockSpec((1, 1, d), lambda i, k: (i // tiles_per_batch, 0, 0))],
        out_specs=pl.BlockSpec((tm, d), lambda i, k: (i, 0)),
        out_shape=jax.ShapeDtypeStruct((m, d), F32),
        scratch_shapes=[pltpu.VMEM((tm, d), F32)],
        compiler_params=_params(("parallel", "arbitrary")),
        name=name,
    )(x, w, h, gain, gate)


def _dn_prep_kernel(cur_ref, prev_ref, next_ref, w_ref, q_ref, k_ref, v_ref, ext_ref, *, n_chunks):
    n = pl.program_id(1)
    keep_prev = jnp.where(n > 0, 1.0, 0.0)
    keep_next = jnp.where(n < n_chunks - 1, 1.0, 0.0)
    ext_ref[0:HALO, :] = prev_ref[...].astype(F32) * keep_prev
    ext_ref[HALO:HALO + CHUNK, :] = cur_ref[...].astype(F32)
    ext_ref[HALO + CHUNK:2 * HALO + CHUNK, :] = next_ref[...].astype(F32) * keep_next
    outs = (q_ref, k_ref, v_ref)
    for slab in range(3 * DN_HEADS):
        sl = slice(slab * DN_HEAD_DIM, (slab + 1) * DN_HEAD_DIM)
        acc = None
        for j in range(CONV_K):
            start = HALO - CONV_K // 2 + j
            term = ext_ref[start:start + CHUNK, sl] * w_ref[j:j + 1, sl]
            acc = term if acc is None else acc + term
        y = acc * _sigmoid(acc)
        which, head = divmod(slab, DN_HEADS)
        if which < 2:
            y = y * lax.rsqrt(jnp.sum(y * y, axis=-1, keepdims=True) + NORM_EPS)
        if which == 0:
            y = y * DN_HEAD_DIM ** -0.5
        outs[which][:, head * DN_HEAD_DIM:(head + 1) * DN_HEAD_DIM] = y.astype(BF16)


def _dn_prep(p, conv_w, *, col_major):
    b, n_chunks = p.shape[:2]
    clamp_prev = lambda n: jnp.maximum(n - 1, 0)
    clamp_next = lambda n: jnp.minimum(n + 1, n_chunks - 1)
    out_spec = _chunk_spec(DN_WIDTH, False, lambda n: n)
    p = p.reshape(_by_column(p.shape, col_major))
    last_halo = CHUNK // HALO - 1

    def halo_spec(chunk_of, row_block):
        if col_major:
            return pl.BlockSpec((None, HALO, QKV_W), lambda i, n: (i, row_block, chunk_of(n)))
        return pl.BlockSpec((None, None, HALO, QKV_W), lambda i, n: (i, chunk_of(n), row_block, 0))

    out_sds = jax.ShapeDtypeStruct((b, n_chunks, CHUNK, DN_WIDTH), BF16)
    return pl.pallas_call(
        functools.partial(_dn_prep_kernel, n_chunks=n_chunks),
        grid=(b, n_chunks),
        in_specs=[_chunk_spec(QKV_W, col_major, lambda n: n), halo_spec(clamp_prev, last_halo),
                  halo_spec(clamp_next, 0), _const_spec((8, QKV_W))],
        out_specs=[out_spec] * 3,
        out_shape=[out_sds] * 3,
        scratch_shapes=[pltpu.VMEM((CHUNK + 2 * HALO, QKV_W), F32)],
        compiler_params=_params(("parallel", "arbitrary")),
        name="dn_prep_lat" if col_major else "dn_prep_ctx",
    )(p, p, p, conv_w)


def _dn_kernel(q_ref, k_ref, v_ref, small_ref, nega_ref, dtb_ref, s0_ref, o_ref, s_ref, *, reverse):
    @pl.when(pl.program_id(0) == 0)
    def _():
        s_ref[...] = s0_ref[...]

    strict, incl, same_block = _chunk_masks(reverse)
    tri, tri_t = _cumsum_mats(reverse)
    d = 1 if reverse else 0
    n_batch = q_ref.shape[0]
    beta_all, gcum_col, gcum_row, gtot_col = [], [], [], []
    for b in range(n_batch):
        gates = small_ref[b, :, DN_GATE_OFF:DN_GATE_OFF + 4 * DN_HEADS]
        beta_all.append(_sigmoid(gates[:, d * DN_HEADS:(d + 1) * DN_HEADS]))
        gcol = nega_ref[...] * _softplus(gates[:, (2 + d) * DN_HEADS:(3 + d) * DN_HEADS] + dtb_ref[...])
        gcum_col.append(_mm_exact_lhs(tri, gcol))
        gcum_row.append(_mm_exact_tn(gcol, tri_t))
        gtot_col.append(jnp.sum(gcol, axis=0, keepdims=True))

    ents = [(b, h) for b in range(n_batch) for h in range(DN_HEADS)]
    idx = range(len(ents))
    sls = [slice(h * DN_HEAD_DIM, (h + 1) * DN_HEAD_DIM) for _, h in ents]
    qs = [q_ref[b, :, sls[i]].astype(F32) for i, (b, h) in enumerate(ents)]
    ks = [k_ref[b, :, sls[i]].astype(F32) for i, (b, h) in enumerate(ents)]
    vs = [v_ref[b, :, sls[i]].astype(F32) for i, (b, h) in enumerate(ents)]
    gcs = [gcum_col[b][:, h:h + 1] for b, h in ents]
    gts = [gtot_col[b][:, h:h + 1] for b, h in ents]
    betas = [beta_all[b][:, h:h + 1] for b, h in ents]
    decs = [jnp.where(incl, jnp.exp(jnp.where(incl, gcs[i] - gcum_row[b][h:h + 1, :], 0.0)), 0.0)
            for i, (b, h) in enumerate(ents)]
    kbs = [ks[i] * betas[i] for i in idx]
    kq = [_mm_nt(jnp.concatenate([kbs[i], qs[i]], axis=0), ks[i]) for i in idx]
    a_mats = [jnp.where(strict, kq[i][:CHUNK] * decs[i], 0.0) for i in idx]
    qks = [jnp.where(incl, kq[i][CHUNK:] * decs[i], 0.0) for i in idx]
    t_corr = _unit_tri_inverse_minus_eye(a_mats, same_block)
    egs = [jnp.exp(gc) for gc in gcs]
    rhs = [jnp.concatenate([vs[i] * betas[i], kbs[i] * egs[i]], axis=1) for i in idx]
    sols = [rhs[i] + _mm(t_corr[i], rhs[i]) for i in idx]
    states = [s_ref[b, h] for b, h in ents]
    ws = [_mm(jnp.concatenate([sols[i][:, DN_HEAD_DIM:], qs[i] * egs[i]], axis=0), states[i])
          for i in idx]
    v_new = [sols[i][:, :DN_HEAD_DIM] - ws[i][:CHUNK] for i in idx]
    for i, (b, h) in enumerate(ents):
        o_ref[b, :, sls[i]] = (ws[i][CHUNK:] + _mm(qks[i], v_new[i])).astype(o_ref.dtype)
    for i, (b, h) in enumerate(ents):
        k_dec = ks[i] * jnp.exp(gts[i] - gcs[i])
        s_ref[b, h] = states[i] * jnp.exp(gts[i]) + _mm_tn(k_dec, v_new[i])


def _dn_scan(q, k, v, p_small, neg_a, dt_bias, s0, *, col_major, reverse):
    b, n_chunks = q.shape[:2]
    order = (lambda n: n_chunks - 1 - n) if reverse else (lambda n: n)
    tok_spec = _scan_spec(b, DN_WIDTH, False, order)
    state_spec = _const_spec(s0.shape)
    o_shape = (b, n_chunks, CHUNK, DN_WIDTH)
    o, s = pl.pallas_call(
        functools.partial(_dn_kernel, reverse=reverse),
        grid=(n_chunks,),
        in_specs=[tok_spec, tok_spec, tok_spec, _scan_spec(b, SMALL_W, col_major, order),
                  _const_spec((1, DN_HEADS)), _const_spec((1, DN_HEADS)), state_spec],
        out_specs=[_scan_spec(b, DN_WIDTH, col_major, order), state_spec],
        out_shape=[jax.ShapeDtypeStruct(_by_column(o_shape, col_major), BF16),
                   jax.ShapeDtypeStruct(s0.shape, F32)],
        compiler_params=_params(("arbitrary",)),
        name=("dn_scan_rev" if reverse else "dn_scan_fwd") + ("_lat" if col_major else "_ctx"),
    )(q, k, v, p_small.reshape(_by_column(p_small.shape, col_major)), neg_a, dt_bias, s0)
    return o.reshape(o_shape), s


def _shifted(cur, prev, nxt, n, n_chunks, lat):
    rows = lax.broadcasted_iota(jnp.int32, cur.shape, 0)
    lane = lax.broadcasted_iota(jnp.int32, cur.shape, 1)
    has_prev = jnp.where(n > 0, 1.0, 0.0)
    has_next = jnp.where(n < n_chunks - 1, 1.0, 0.0)
    back = pltpu.roll(cur, 1, axis=0)
    fwd = pltpu.roll(cur, CHUNK - 1, axis=0)
    if lat:
        left = jnp.where(rows == 0, 0.0, back)
        right = jnp.where(rows == CHUNK - 1, 0.0, fwd)
        sel = lane % 4
        return jnp.where(sel == 0, left, jnp.where(sel == 1, right,
                         jnp.where(sel == 2, prev * has_prev, nxt * has_next)))
    before = jnp.where(rows == 0, prev[CHUNK - 1:CHUNK, :] * has_prev, back)
    after = jnp.where(rows == CHUNK - 1, nxt[0:1, :] * has_next, fwd)
    return jnp.where(lane % 2 == 0, before, after)


def _rw_prep_kernel(big_ref, bigp_ref, bign_ref, sm_ref, smp_ref, smn_ref, mub_ref, mus_ref,
                    wup_ref, aup_ref, gup_ref, w0_ref, a0_ref, kk_ref, ka_ref, rk_ref, seg_ref,
                    r_ref, v_ref, a_ref, gate_ref, bonus_ref, kd0_ref, kd1_ref, b0_ref, b1_ref,
                    lw0_ref, lw1_ref, *, n_chunks, lat):
    n = pl.program_id(1)
    big = big_ref[...].astype(F32)
    big_shift = _shifted(big, bigp_ref[...].astype(F32), bign_ref[...].astype(F32), n, n_chunks, lat)
    big = big + mub_ref[...] * (big_shift - big)
    sm = sm_ref[...]
    sm = sm + mus_ref[...] * (_shifted(sm, smp_ref[...], smn_ref[...], n, n_chunks, lat) - sm)
    r = big[:, :RW_WIDTH]
    k = big[:, RW_WIDTH:2 * RW_WIDTH]
    v = big[:, 2 * RW_WIDTH:]
    r_ref[...] = r.astype(BF16)
    v_ref[...] = v.astype(BF16)
    gate_ref[...] = _mm(_sigmoid(sm[:, GD_OFF:GD_OFF + G_LORA]), gup_ref[...]).astype(BF16)
    kx = k * kk_ref[...]
    kk = kx * lax.rsqrt(_seg_sum(kx * kx, seg_ref) + NORM_EPS)
    a_ref[...] = (-kk).astype(BF16)
    bonus = None
    for d, (kd_ref, b_ref, lw_ref) in enumerate(((kd0_ref, b0_ref, lw0_ref), (kd1_ref, b1_ref, lw1_ref))):
        wd = jnp.tanh(sm[:, d * W_LORA:(d + 1) * W_LORA])
        ad = sm[:, 2 * W_LORA + d * A_LORA:2 * W_LORA + (d + 1) * A_LORA]
        log_w = -_softplus(-(w0_ref[d:d + 1, :] + _mm(wd, wup_ref[d]))) - 0.5
        lw_ref[...] = -jnp.exp(log_w)
        iclr = _sigmoid(a0_ref[d:d + 1, :] + _mm(ad, aup_ref[d]))
        k_d = k * (1.0 + (iclr - 1.0) * ka_ref[...])
        kd_ref[...] = k_d.astype(BF16)
        b_ref[...] = (kk * iclr).astype(BF16)
        term = _seg_sum(r * k_d * rk_ref[...], seg_ref)
        bonus = term if bonus is None else bonus + term
    bonus_ref[...] = (bonus * v).astype(BF16)


def _rw_prep(p_rw, p_small, mu_big, mu_small, w_up, a_up, g_up, w0, a0, k_k, k_a, r_k, seg, *, lat):
    b, n_chunks = p_rw.shape[:2]
    clamp_prev = lambda n: jnp.maximum(n - 1, 0)
    clamp_next = lambda n: jnp.minimum(n + 1, n_chunks - 1)
    big = lambda f: _chunk_spec(QKV_W, False, f)
    small = lambda f: _chunk_spec(SMALL_W, False, f)
    same = lambda n: n
    out_spec = _chunk_spec(RW_WIDTH, False, same)
    out_sds = lambda dt: jax.ShapeDtypeStruct((b, n_chunks, CHUNK, RW_WIDTH), dt)
    row = lambda w: _const_spec((1, w))
    return pl.pallas_call(
        functools.partial(_rw_prep_kernel, n_chunks=n_chunks, lat=lat),
        grid=(b, n_chunks),
        in_specs=[big(same), big(clamp_prev), big(clamp_next),
                  small(same), small(clamp_prev), small(clamp_next),
                  row(QKV_W), row(SMALL_W),
                  _const_spec((2, W_LORA, RW_WIDTH)), _const_spec((2, A_LORA, RW_WIDTH)),
                  _const_spec((G_LORA, RW_WIDTH)), _const_spec((2, RW_WIDTH)), _const_spec((2, RW_WIDTH)),
                  row(RW_WIDTH), row(RW_WIDTH), row(RW_WIDTH), _const_spec((SEG_TILE, SEG_TILE))],
        out_specs=[out_spec] * 11,
        out_shape=[out_sds(BF16)] * 9 + [out_sds(F32)] * 2,
        compiler_params=_params(("parallel", "arbitrary")),
        name="rw_prep_lat" if lat else "rw_prep_ctx",
    )(p_rw, p_rw, p_rw, p_small, p_small, p_small, mu_big, mu_small, w_up, a_up, g_up, w0, a0, k_k, k_a, r_k, seg)


def _rw_kernel(r_ref, lw_ref, k_ref, v_ref, a_ref, b_ref, s0_ref, y_ref, s_ref, *, reverse):
    @pl.when(pl.program_id(0) == 0)
    def _():
        s_ref[...] = s0_ref[...]

    strict, _, same_block = _chunk_masks(reverse)
    _, incl2, _ = _chunk_masks(reverse, 2 * CHUNK)
    tri, _ = _cumsum_mats(reverse)
    n_batch = r_ref.shape[0]
    a_t, r_t, b_t, k_t, b_e, k_e, v_all, w_tot = ([] for _ in range(8))
    for b in range(n_batch):
        lw = lw_ref[b]
        cum = _mm_exact_lhs(tri, lw)
        tot = jnp.sum(lw, axis=0, keepdims=True)
        w_inv = jnp.exp(-cum)
        w_rest = jnp.exp(tot - cum)
        w_tot.append(jnp.exp(tot))
        a_t.append(a_ref[b].astype(F32) * jnp.exp(cum - lw))
        r_t.append(r_ref[b].astype(F32) * jnp.exp(cum))
        b_all = b_ref[b].astype(F32)
        k_all = k_ref[b].astype(F32)
        v_all.append(v_ref[b].astype(F32))
        b_t.append(b_all * w_inv)
        k_t.append(k_all * w_inv)
        b_e.append(b_all * w_rest)
        k_e.append(k_all * w_rest)

    ents = [(b, h) for b in range(n_batch) for h in range(RW_HEADS)]
    idx = range(len(ents))
    sls = [slice(h * RW_HEAD_DIM, (h + 1) * RW_HEAD_DIM) for _, h in ents]
    ahs = [a_t[b][:, sls[i]] for i, (b, h) in enumerate(ents)]
    rhs_ = [r_t[b][:, sls[i]] for i, (b, h) in enumerate(ents)]
    vhs = [v_all[b][:, sls[i]] for i, (b, h) in enumerate(ents)]
    gs = [_mm_nt(jnp.concatenate([ahs[i], rhs_[i]], axis=0),
                 jnp.concatenate([b_t[b][:, sls[i]], k_t[b][:, sls[i]]], axis=0))
          for i, (b, h) in enumerate(ents)]
    neg_ab = [jnp.where(strict, -g[:CHUNK, :CHUNK], 0.0) for g in gs]
    a_ak = [jnp.where(strict, g[:CHUNK, CHUNK:], 0.0) for g in gs]
    m_r = [jnp.where(incl2, g[CHUNK:, :], 0.0) for g in gs]
    t_corr = _unit_tri_inverse_minus_eye(neg_ab, same_block)
    rhs = [jnp.concatenate([ahs[i], _mm(a_ak[i], vhs[i])], axis=1) for i in idx]
    sols = [rhs[i] + _mm(t_corr[i], rhs[i]) for i in idx]
    states = [s_ref[b, h] for b, h in ents]
    lhs = [_mm_nt(jnp.concatenate([sols[i][:, :RW_HEAD_DIM], rhs_[i]], axis=0), states[i])
           for i in idx]
    uvs = [jnp.concatenate([lhs[i][:CHUNK] + sols[i][:, RW_HEAD_DIM:], vhs[i]], axis=0) for i in idx]
    ys = [lhs[i][CHUNK:] + _mm(m_r[i], uvs[i]) for i in idx]
    for i in range(0, len(ents), 2):
        b, h = ents[i]
        y_ref[b, :, h * RW_HEAD_DIM:(h + 2) * RW_HEAD_DIM] = jnp.concatenate(
            [ys[i], ys[i + 1]], axis=1).astype(y_ref.dtype)
    for i, (b, h) in enumerate(ents):
        bk = jnp.concatenate([b_e[b][:, sls[i]], k_e[b][:, sls[i]]], axis=0)
        s_ref[b, h] = states[i] * w_tot[b][:, sls[i]] + _mm_tn(uvs[i], bk)


def _rw_scan(r, lw, k, v, a, bvec, s0, *, reverse, tag):
    b, n_chunks = r.shape[:2]
    order = (lambda n: n_chunks - 1 - n) if reverse else (lambda n: n)
    tok_spec = _scan_spec(b, RW_WIDTH, False, order)
    state_spec = _const_spec(s0.shape)
    return pl.pallas_call(
        functools.partial(_rw_kernel, reverse=reverse),
        grid=(n_chunks,),
        in_specs=[tok_spec] * 6 + [state_spec],
        out_specs=[tok_spec, state_spec],
        out_shape=[jax.ShapeDtypeStruct(r.shape, BF16), jax.ShapeDtypeStruct(s0.shape, F32)],
        compiler_params=_params(("arbitrary",)),
        name=("rw_scan_rev_" if reverse else "rw_scan_fwd_") + tag,
    )(r, lw, k, v, a, bvec, s0)


def _mix_post_kernel(of_ref, or_ref, z_ref, yf_ref, yr_ref, bonus_ref, gate_ref, og_ref, lnw_ref,
                     lnb_ref, seg_ref, o_ref):
    for h in range(DN_HEADS):
        sl = slice(h * DN_HEAD_DIM, (h + 1) * DN_HEAD_DIM)
        o = of_ref[:, sl].astype(F32) + or_ref[:, sl].astype(F32)
        o = o * lax.rsqrt(jnp.mean(o * o, axis=-1, keepdims=True) + NORM_EPS) * og_ref[...]
        z = z_ref[:, sl].astype(F32)
        o_ref[:, sl] = (o * (z * _sigmoid(z))).astype(o_ref.dtype)
    y = yf_ref[...].astype(F32) + yr_ref[...].astype(F32)
    dev = y - _seg_sum(y, seg_ref) * (1.0 / RW_HEAD_DIM)
    var = _seg_sum(dev * dev, seg_ref) * (1.0 / RW_HEAD_DIM)
    yn = dev * lax.rsqrt(var + RW_LN_EPS)
    out = (yn * lnw_ref[...] + lnb_ref[...] + bonus_ref[...].astype(F32)) * gate_ref[...].astype(F32)
    o_ref[:, DN_WIDTH:] = out.astype(o_ref.dtype)


def _mix_post(o_f, o_r, z, y_f, y_r, bonus, gate, out_g, ln_w, ln_b, seg):
    b, n_chunks = o_f.shape[:2]
    same = lambda n: n
    tok = _chunk_spec(DN_WIDTH, False, same)
    row = lambda w: _const_spec((1, w))
    return pl.pallas_call(
        _mix_post_kernel,
        grid=(b, n_chunks),
        in_specs=[tok] * 7 + [row(DN_HEAD_DIM), row(RW_WIDTH), row(RW_WIDTH),
                              _const_spec((SEG_TILE, SEG_TILE))],
        out_specs=_chunk_spec(D_MODEL, False, same),
        out_shape=jax.ShapeDtypeStruct((b, n_chunks, CHUNK, D_MODEL), BF16),
        compiler_params=_params(("parallel", "arbitrary")),
        name="mix_post",
    )(o_f, o_r, z, y_f, y_r, bonus, gate, out_g, ln_w, ln_b, seg)


def _split_w_in(w):
    dn_qkv, dn_z, dn_gates = w[:, :QKV_W], w[:, QKV_W:4 * DN_WIDTH], w[:, 4 * DN_WIDTH:P_DN]
    rw_rkv, rw_small = w[:, P_DN:P_DN + QKV_W], w[:, P_DN + QKV_W:]
    pad = jnp.zeros((w.shape[0], SMALL_W - DN_GATE_OFF - 4 * DN_HEADS), w.dtype)
    return rw_rkv, dn_qkv, dn_z, jnp.concatenate([rw_small, dn_gates, pad], axis=1)


def kernel(x, c, ctx, c_ctx, w_mod, b_mod, mix_pre_g, mix_post_g, ffn_pre_g, ffn_post_g, w_in,
           dn_conv, dn_a_log, dn_dt_bias, dn_out_g, rw_mu, rw_w0, rw_w_up, rw_a0, rw_a_up, rw_g_up,
           rw_k_k, rw_k_a, rw_r_k, rw_ln_w, rw_ln_b, w_out, w_ffn_in, w_ffn_out):
    bsz, seq, d = x.shape
    n_ctx = ctx.shape[1]
    rows = seq // GRID_W
    assert GRID_W == CHUNK and rows == CHUNK and n_ctx % CHUNK == 0

    cond = jnp.concatenate([jax.nn.silu(c), jax.nn.silu(c_ctx)[None]], axis=0)
    cond = jnp.pad(cond, ((0, 8 - cond.shape[0]), (0, 0)))

    head_ids = jnp.arange(SEG_TILE) // RW_HEAD_DIM
    seg = (head_ids[:, None] == head_ids[None, :]).astype(BF16)

    streams = [
        dict(h=ctx.reshape(bsz, n_ctx // CHUNK, CHUNK, d), lat=False, tm=256, tm_pro=256),
        dict(h=x.reshape(bsz, rows, GRID_W, d), lat=True, tm=512, tm_pro=1024),
    ]

    for l in range(DEPTH):
        last = l == DEPTH - 1
        mod_l = _matmul(cond, w_mod[l], tm=8, tn=1536, name="adaln") + b_mod[l]
        mod_l = mod_l.reshape(8, 6, d)
        mods = [mod_l[bsz:bsz + 1, :, None, :], mod_l[:bsz, :, None, :]]

        w_in_groups = [w.astype(BF16) for w in _split_w_in(w_in[l])]
        conv_w = jnp.pad(dn_conv[l], ((0, 8 - CONV_K), (0, 0)))
        mu = rw_mu[l]
        mu_big = mu[None, :QKV_W]
        mu_small = jnp.pad(mu[QKV_W:], (0, SMALL_W - (P_RW - QKV_W)))[None]
        w_up = rw_w_up[l].astype(BF16)
        a_up = rw_a_up[l].astype(BF16)
        g_up = rw_g_up[l].astype(BF16)
        neg_a = -jnp.exp(dn_a_log[l])
        dt_b = dn_dt_bias[l]
        out_g = dn_out_g[l][None]
        w_out_l = w_out[l].astype(BF16)
        w_i = w_ffn_in[l].astype(BF16)
        w_o = w_ffn_out[l].astype(BF16)

        dn_state = [jnp.zeros((bsz, DN_HEADS, DN_HEAD_DIM, DN_HEAD_DIM), F32) for _ in range(2)]
        rw_state = [jnp.zeros((bsz, RW_HEADS, RW_HEAD_DIM, RW_HEAD_DIM), F32) for _ in range(2)]

        for si, st in enumerate(streams):
            h = st["h"]
            lat, tm, tm_pro = st["lat"], st["tm"], st["tm_pro"]
            nb = bsz if lat else 1
            n_chunks = h.shape[1]
            m = bsz * n_chunks * CHUNK
            mod = mods[si]
            tag = "lat" if lat else "ctx"

            p_rw, p_dn, p_z, p_small = [
                _norm_mod_matmul(h.reshape(m, d), mix_pre_g[l][None], mod[:, 0], mod[:, 1], w,
                                 swiglu=False, tm=tm_pro, tn=min(w.shape[1], 1024),
                                 out_dtype=F32 if w.shape[1] == SMALL_W else BF16,
                                 name="in_proj_" + tag).reshape(bsz, n_chunks, CHUNK, w.shape[1])
                for w in w_in_groups]

            q, k, v = _dn_prep(p_dn, conv_w, col_major=lat)
            dn_o = []
            for dr in range(2):
                o, dn_state[dr] = _dn_scan(q, k, v, p_small, neg_a[dr][None], dt_b[dr][None], dn_state[dr],
                                           col_major=lat, reverse=(dr == 1))
                dn_o.append(o)

            (r, vr, a, gate, bonus, kd0, kd1, b0, b1, lw0, lw1) = _rw_prep(
                p_rw, p_small, mu_big, mu_small, w_up, a_up, g_up, rw_w0[l], rw_a0[l], rw_k_k[l][None],
                rw_k_a[l][None], rw_r_k[l].reshape(1, RW_WIDTH), seg, lat=lat)
            rw_y = []
            for dr, (kd, bv, lw) in enumerate(((kd0, b0, lw0), (kd1, b1, lw1))):
                y, rw_state[dr] = _rw_scan(r, lw, kd, vr, a, bv, rw_state[dr], reverse=(dr == 1), tag=tag)
                rw_y.append(y)

            if last and not lat:
                continue

            mix = _mix_post(dn_o[0], dn_o[1], p_z, rw_y[0], rw_y[1], bonus, gate, out_g,
                            rw_ln_w[l][None], rw_ln_b[l][None], seg)
            h2 = _matmul_residual(mix.reshape(m, d), w_out_l, h.reshape(m, d), mix_post_g[l][None],
                                  mod[:, 2], tm=tm, tk=d, name="out_proj_" + tag)

            hid = _norm_mod_matmul(h2, ffn_pre_g[l][None], mod[:, 3], mod[:, 4], w_i, swiglu=True,
                                   tm=tm_pro, tn=512, out_dtype=BF16, name="ffn_in_" + tag)
            h3 = _matmul_residual(hid, w_o, h2, ffn_post_g[l][None], mod[:, 5], tm=tm,
                                  tk=FFN_HIDDEN // 4, name="ffn_out_" + tag)
            st["h"] = h3.reshape(bsz, n_chunks, CHUNK, d)

    return streams[1]["h"].reshape(bsz, seq, d)
```

```python
import functools

import jax
import jax.numpy as jnp
from jax import lax
from jax.experimental import pallas as pl
from jax.experimental.pallas import tpu as pltpu

D_MODEL = 2048
DEPTH = 4
GRID_W = 64
DN_WIDTH = 1024
DN_HEAD_DIM = 128
DN_HEADS = 8
RW_WIDTH = 1024
RW_HEAD_DIM = 64
RW_HEADS = 16
CONV_K = 7
CHUNK = 64
W_LORA = 64
A_LORA = 64
G_LORA = 160
FFN_HIDDEN = 5632
P_DN = 4 * DN_WIDTH + 4 * DN_HEADS
P_RW = 3 * RW_WIDTH + 2 * W_LORA + 2 * A_LORA + G_LORA
NORM_EPS = 1e-6
RW_LN_EPS = 64e-5

QKV_W = 3 * 1024
SMALL_W = 512
GD_OFF = 2 * W_LORA + 2 * A_LORA
DN_GATE_OFF = GD_OFF + G_LORA
TRI_BLOCK = 16
HALO = 16
SEG_TILE = 256
VMEM_LIMIT = 48 * 1024 * 1024

F32 = jnp.float32
BF16 = jnp.bfloat16


def _mm(a, b):
    return jnp.dot(a.astype(BF16), b.astype(BF16), preferred_element_type=F32)


def _mm_nt(a, b):
    return lax.dot_general(a.astype(BF16), b.astype(BF16), (((1,), (1,)), ((), ())),
                           preferred_element_type=F32)


def _mm_tn(a, b):
    return lax.dot_general(a.astype(BF16), b.astype(BF16), (((0,), (0,)), ((), ())),
                           preferred_element_type=F32)


def _split3(x):
    hi = x.astype(BF16)
    r1 = x - hi.astype(F32)
    mid = r1.astype(BF16)
    lo = (r1 - mid.astype(F32)).astype(BF16)
    return hi, mid, lo


def _mm_exact_lhs(a_bf16, b):
    return sum(jnp.dot(a_bf16, part, preferred_element_type=F32) for part in _split3(b))


def _mm_exact_tn(a, b_bf16):
    return sum(lax.dot_general(part, b_bf16, (((0,), (0,)), ((), ())), preferred_element_type=F32)
               for part in _split3(a))


def _seg_sum(x, seg_ref):
    hi = x.astype(BF16)
    lo = (x - hi.astype(F32)).astype(BF16)
    seg = seg_ref[...]
    groups = []
    for g in range(x.shape[1] // SEG_TILE):
        sl = slice(g * SEG_TILE, (g + 1) * SEG_TILE)
        groups.append(jnp.dot(hi[:, sl], seg, preferred_element_type=F32)
                      + jnp.dot(lo[:, sl], seg, preferred_element_type=F32))
    return jnp.concatenate(groups, axis=1)


def _softplus(x):
    return jnp.maximum(x, 0.0) + jnp.log(1.0 + jnp.exp(-jnp.abs(x)))


def _sigmoid(x):
    return 1.0 / (1.0 + jnp.exp(-x))


def _chunk_masks(reverse, width=CHUNK):
    row = lax.broadcasted_iota(jnp.int32, (CHUNK, width), 0)
    col = lax.broadcasted_iota(jnp.int32, (CHUNK, width), 1) % CHUNK
    if reverse:
        strict, incl = row < col, row <= col
    else:
        strict, incl = row > col, row >= col
    same_block = (row // TRI_BLOCK) == (col // TRI_BLOCK)
    return strict, incl, same_block


def _cumsum_mats(reverse):
    row = lax.broadcasted_iota(jnp.int32, (CHUNK, CHUNK), 0)
    col = lax.broadcasted_iota(jnp.int32, (CHUNK, CHUNK), 1)
    lower, upper = (row >= col).astype(BF16), (row <= col).astype(BF16)
    return (upper, lower) if reverse else (lower, upper)


def _unit_tri_inverse_minus_eye(mats, same_block):
    a_d = [jnp.where(same_block, a, 0.0) for a in mats]
    a_o = [a - d for a, d in zip(mats, a_d)]
    xe = [-d for d in a_d]
    p = [_mm(d, d) for d in a_d]
    for level in range(3):
        if level < 2:
            both = [_mm(jnp.concatenate([x, q], axis=0), q) for x, q in zip(xe, p)]
            xp = [r[:CHUNK] for r in both]
            p_next = [r[CHUNK:] for r in both]
        else:
            xp = [_mm(x, q) for x, q in zip(xe, p)]
            p_next = None
        xe = [x + q + r for x, q, r in zip(xe, p, xp)]
        p = p_next
    n = [o + _mm(x, o) for x, o in zip(xe, a_o)]
    n2 = [_mm(m, m) for m in n]
    n3 = [_mm(m, m2) for m, m2 in zip(n, n2)]
    ye = [m2 - m - m3 for m, m2, m3 in zip(n, n2, n3)]
    yx = [_mm(y, x) for y, x in zip(ye, xe)]
    return [y + x + r for y, x, r in zip(ye, xe, yx)]


def _params(semantics):
    return pltpu.CompilerParams(dimension_semantics=semantics, vmem_limit_bytes=VMEM_LIMIT)


def _chunk_spec(width, col_major, chunk_of):
    if col_major:
        return pl.BlockSpec((None, CHUNK, width), lambda b, n: (b, 0, chunk_of(n)))
    return pl.BlockSpec((None, None, CHUNK, width), lambda b, n: (b, chunk_of(n), 0, 0))


def _by_column(shape, col_major):
    return (shape[0], shape[1], shape[2] * shape[3]) if col_major else tuple(shape)


def _scan_spec(batch, width, col_major, chunk_of):
    if col_major:
        return pl.BlockSpec((batch, CHUNK, width), lambda n: (0, 0, chunk_of(n)))
    return pl.BlockSpec((batch, None, CHUNK, width), lambda n: (0, chunk_of(n), 0, 0))


def _const_spec(shape):
    return pl.BlockSpec(shape, lambda *_: (0,) * len(shape))


def _mm_kernel(x_ref, w_ref, o_ref):
    o_ref[...] = _mm(x_ref[...], w_ref[...]).astype(o_ref.dtype)


def _matmul(x, w, *, tm, tn, name, out_dtype=F32):
    m, k = x.shape
    n = w.shape[1]
    assert m % tm == 0 and n % tn == 0, (m, n, tm, tn)
    return pl.pallas_call(
        _mm_kernel,
        grid=(m // tm, n // tn),
        in_specs=[pl.BlockSpec((tm, k), lambda i, j: (i, 0)),
                  pl.BlockSpec((k, tn), lambda i, j: (0, j))],
        out_specs=pl.BlockSpec((tm, tn), lambda i, j: (i, j)),
        out_shape=jax.ShapeDtypeStruct((m, n), out_dtype),
        compiler_params=_params(("parallel", "arbitrary")),
        name=name,
    )(x, w)


def _norm_mod_rows_kernel(x_ref, g_ref, sh_ref, sc_ref, o_ref):
    x = x_ref[...]
    y = x * lax.rsqrt(jnp.mean(x * x, axis=-1, keepdims=True) + NORM_EPS) * g_ref[...]
    o_ref[...] = (y * (1.0 + sc_ref[0]) + sh_ref[0]).astype(o_ref.dtype)


def _norm_mod_rows(x, gain, shift, scale, *, tm, name):
    m, d = x.shape
    nb = shift.shape[0]
    tiles_per_batch = m // tm // nb
    assert m % (tm * nb) == 0
    mod_spec = pl.BlockSpec((1, 1, d), lambda i: (i // tiles_per_batch, 0, 0))
    return pl.pallas_call(
        _norm_mod_rows_kernel,
        grid=(m // tm,),
        in_specs=[pl.BlockSpec((tm, d), lambda i: (i, 0)), pl.BlockSpec((1, d), lambda i: (0, 0)),
                  mod_spec, mod_spec],
        out_specs=pl.BlockSpec((tm, d), lambda i: (i, 0)),
        out_shape=jax.ShapeDtypeStruct((m, d), BF16),
        compiler_params=_params(("parallel",)),
        name=name,
    )(x, gain, shift, scale)


def _norm_mod_kernel(x_ref, g_ref, sh_ref, sc_ref, *rest, swiglu):
    if swiglu:
        wg_ref, wu_ref, o_ref, u_ref = rest
    else:
        w_ref, o_ref, u_ref = rest

    @pl.when(pl.program_id(1) == 0)
    def _():
        x = x_ref[...]
        y = x * lax.rsqrt(jnp.mean(x * x, axis=-1, keepdims=True) + NORM_EPS) * g_ref[...]
        u_ref[...] = (y * (1.0 + sc_ref[0]) + sh_ref[0]).astype(BF16)

    u = u_ref[...]
    if swiglu:
        g = jnp.dot(u, wg_ref[...], preferred_element_type=F32)
        up = jnp.dot(u, wu_ref[...], preferred_element_type=F32)
        o_ref[...] = (g * _sigmoid(g) * up).astype(o_ref.dtype)
    else:
        o_ref[...] = jnp.dot(u, w_ref[...], preferred_element_type=F32).astype(o_ref.dtype)


def _norm_mod_matmul(x, gain, shift, scale, w, *, swiglu, tm, tn, out_dtype, name):
    m, d = x.shape
    n = w.shape[1] // 2 if swiglu else w.shape[1]
    nb = shift.shape[0]
    tiles_per_batch = m // tm // nb
    n_tiles = n // tn
    assert m % (tm * nb) == 0 and n % tn == 0
    mod_spec = pl.BlockSpec((1, 1, d), lambda i, j: (i // tiles_per_batch, 0, 0))
    w_specs = [pl.BlockSpec((d, tn), lambda i, j: (0, j))]
    if swiglu:
        w_specs.append(pl.BlockSpec((d, tn), lambda i, j: (0, j + n_tiles)))
    return pl.pallas_call(
        functools.partial(_norm_mod_kernel, swiglu=swiglu),
        grid=(m // tm, n_tiles),
        in_specs=[pl.BlockSpec((tm, d), lambda i, j: (i, 0)),
                  pl.BlockSpec((1, d), lambda i, j: (0, 0)), mod_spec, mod_spec] + w_specs,
        out_specs=pl.BlockSpec((tm, tn), lambda i, j: (i, j)),
        out_shape=jax.ShapeDtypeStruct((m, n), out_dtype),
        scratch_shapes=[pltpu.VMEM((tm, d), BF16)],
        compiler_params=_params(("parallel", "arbitrary")),
        name=name,
    )(x, gain, shift, scale, *([w, w] if swiglu else [w]))


def _mm_resid_kernel(x_ref, w_ref, h_ref, g_ref, gate_ref, o_ref, acc_ref, *, n_k):
    k = pl.program_id(1)

    @pl.when(k == 0)
    def _():
        acc_ref[...] = jnp.zeros_like(acc_ref)

    acc_ref[...] += jnp.dot(x_ref[...], w_ref[...], preferred_element_type=F32)

    @pl.when(k == n_k - 1)
    def _():
        y = acc_ref[...]
        y = y * lax.rsqrt(jnp.mean(y * y, axis=-1, keepdims=True) + NORM_EPS) * g_ref[...]
        o_ref[...] = h_ref[...] + gate_ref[0] * y


def _matmul_residual(x, w, h, gain, gate, *, tm, tk, name):
    m, kdim = x.shape
    d = w.shape[1]
    nb = gate.shape[0]
    tiles_per_batch = m // tm // nb
    n_k = kdim // tk
    assert m % (tm * nb) == 0 and kdim % tk == 0
    return pl.pallas_call(
        functools.partial(_mm_resid_kernel, n_k=n_k),
        grid=(m // tm, n_k),
        in_specs=[pl.BlockSpec((tm, tk), lambda i, k: (i, k)),
                  pl.BlockSpec((tk, d), lambda i, k: (k, 0)),
                  pl.BlockSpec((tm, d), lambda i, k: (i, 0)),
                  pl.BlockSpec((1, d), lambda i, k: (0, 0)),
                  pl.BlockSpec((1, 1, d), lambda i, k: (i // tiles_per_batch, 0, 0))],
        out_specs=pl.BlockSpec((tm, d), lambda i, k: (i, 0)),
        out_shape=jax.ShapeDtypeStruct((m, d), F32),
        scratch_shapes=[pltpu.VMEM((tm, d), F32)],
        compiler_params=_params(("parallel", "arbitrary")),
        name=name,
    )(x, w, h, gain, gate)


def _dn_prep_kernel(cur_ref, prev_ref, next_ref, w_ref, q_ref, k_ref, v_ref, ext_ref, *, n_chunks):
    n = pl.program_id(1)
    keep_prev = jnp.where(n > 0, 1.0, 0.0)
    keep_next = jnp.where(n < n_chunks - 1, 1.0, 0.0)
    ext_ref[0:HALO, :] = prev_ref[...].astype(F32) * keep_prev
    ext_ref[HALO:HALO + CHUNK, :] = cur_ref[...].astype(F32)
    ext_ref[HALO + CHUNK:2 * HALO + CHUNK, :] = next_ref[...].astype(F32) * keep_next
    outs = (q_ref, k_ref, v_ref)
    for slab in range(3 * DN_HEADS):
        sl = slice(slab * DN_HEAD_DIM, (slab + 1) * DN_HEAD_DIM)
        acc = None
        for j in range(CONV_K):
            start = HALO - CONV_K // 2 + j
            term = ext_ref[start:start + CHUNK, sl] * w_ref[j:j + 1, sl]
            acc = term if acc is None else acc + term
        y = acc * _sigmoid(acc)
        which, head = divmod(slab, DN_HEADS)
        if which < 2:
            y = y * lax.rsqrt(jnp.sum(y * y, axis=-1, keepdims=True) + NORM_EPS)
        if which == 0:
            y = y * DN_HEAD_DIM ** -0.5
        outs[which][:, head * DN_HEAD_DIM:(head + 1) * DN_HEAD_DIM] = y.astype(BF16)


def _dn_prep(p, conv_w, *, col_major):
    b, n_chunks = p.shape[:2]
    clamp_prev = lambda n: jnp.maximum(n - 1, 0)
    clamp_next = lambda n: jnp.minimum(n + 1, n_chunks - 1)
    out_spec = _chunk_spec(DN_WIDTH, False, lambda n: n)
    p = p.reshape(_by_column(p.shape, col_major))
    last_halo = CHUNK // HALO - 1

    def halo_spec(chunk_of, row_block):
        if col_major:
            return pl.BlockSpec((None, HALO, QKV_W), lambda i, n: (i, row_block, chunk_of(n)))
        return pl.BlockSpec((None, None, HALO, QKV_W), lambda i, n: (i, chunk_of(n), row_block, 0))

    out_sds = jax.ShapeDtypeStruct((b, n_chunks, CHUNK, DN_WIDTH), BF16)
    return pl.pallas_call(
        functools.partial(_dn_prep_kernel, n_chunks=n_chunks),
        grid=(b, n_chunks),
        in_specs=[_chunk_spec(QKV_W, col_major, lambda n: n), halo_spec(clamp_prev, last_halo),
                  halo_spec(clamp_next, 0), _const_spec((8, QKV_W))],
        out_specs=[out_spec] * 3,
        out_shape=[out_sds] * 3,
        scratch_shapes=[pltpu.VMEM((CHUNK + 2 * HALO, QKV_W), F32)],
        compiler_params=_params(("parallel", "arbitrary")),
        name="dn_prep_lat" if col_major else "dn_prep_ctx",
    )(p, p, p, conv_w)


def _dn_kernel(q_ref, k_ref, v_ref, small_ref, nega_ref, dtb_ref, s0_ref, o_ref, s_ref, *, reverse):
    @pl.when(pl.program_id(0) == 0)
    def _():
        s_ref[...] = s0_ref[...]

    strict, incl, same_block = _chunk_masks(reverse)
    tri, tri_t = _cumsum_mats(reverse)
    d = 1 if reverse else 0
    n_batch = q_ref.shape[0]
    beta_all, gcum_col, gcum_row, gtot_col = [], [], [], []
    for b in range(n_batch):
        gates = small_ref[b, :, DN_GATE_OFF:DN_GATE_OFF + 4 * DN_HEADS]
        beta_all.append(_sigmoid(gates[:, d * DN_HEADS:(d + 1) * DN_HEADS]))
        gcol = nega_ref[...] * _softplus(gates[:, (2 + d) * DN_HEADS:(3 + d) * DN_HEADS] + dtb_ref[...])
        gcum_col.append(_mm_exact_lhs(tri, gcol))
        gcum_row.append(_mm_exact_tn(gcol, tri_t))
        gtot_col.append(jnp.sum(gcol, axis=0, keepdims=True))

    ents = [(b, h) for b in range(n_batch) for h in range(DN_HEADS)]
    idx = range(len(ents))
    sls = [slice(h * DN_HEAD_DIM, (h + 1) * DN_HEAD_DIM) for _, h in ents]
    qs = [q_ref[b, :, sls[i]].astype(F32) for i, (b, h) in enumerate(ents)]
    ks = [k_ref[b, :, sls[i]].astype(F32) for i, (b, h) in enumerate(ents)]
    vs = [v_ref[b, :, sls[i]].astype(F32) for i, (b, h) in enumerate(ents)]
    gcs = [gcum_col[b][:, h:h + 1] for b, h in ents]
    gts = [gtot_col[b][:, h:h + 1] for b, h in ents]
    betas = [beta_all[b][:, h:h + 1] for b, h in ents]
    decs = [jnp.where(incl, jnp.exp(jnp.where(incl, gcs[i] - gcum_row[b][h:h + 1, :], 0.0)), 0.0)
            for i, (b, h) in enumerate(ents)]
    kbs = [ks[i] * betas[i] for i in idx]
    kq = [_mm_nt(jnp.concatenate([kbs[i], qs[i]], axis=0), ks[i]) for i in idx]
    a_mats = [jnp.where(strict, kq[i][:CHUNK] * decs[i], 0.0) for i in idx]
    qks = [jnp.where(incl, kq[i][CHUNK:] * decs[i], 0.0) for i in idx]
    t_corr = _unit_tri_inverse_minus_eye(a_mats, same_block)
    egs = [jnp.exp(gc) for gc in gcs]
    rhs = [jnp.concatenate([vs[i] * betas[i], kbs[i] * egs[i]], axis=1) for i in idx]
    sols = [rhs[i] + _mm(t_corr[i], rhs[i]) for i in idx]
    states = [s_ref[b, h] for b, h in ents]
    ws = [_mm(jnp.concatenate([sols[i][:, DN_HEAD_DIM:], qs[i] * egs[i]], axis=0), states[i])
          for i in idx]
    v_new = [sols[i][:, :DN_HEAD_DIM] - ws[i][:CHUNK] for i in idx]
    for i, (b, h) in enumerate(ents):
        o_ref[b, :, sls[i]] = (ws[i][CHUNK:] + _mm(qks[i], v_new[i])).astype(o_ref.dtype)
    for i, (b, h) in enumerate(ents):
        k_dec = ks[i] * jnp.exp(gts[i] - gcs[i])
        s_ref[b, h] = states[i] * jnp.exp(gts[i]) + _mm_tn(k_dec, v_new[i])


def _dn_scan(q, k, v, p_small, neg_a, dt_bias, s0, *, col_major, reverse):
    b, n_chunks = q.shape[:2]
    order = (lambda n: n_chunks - 1 - n) if reverse else (lambda n: n)
    tok_spec = _scan_spec(b, DN_WIDTH, False, order)
    state_spec = _const_spec(s0.shape)
    o_shape = (b, n_chunks, CHUNK, DN_WIDTH)
    o, s = pl.pallas_call(
        functools.partial(_dn_kernel, reverse=reverse),
        grid=(n_chunks,),
        in_specs=[tok_spec, tok_spec, tok_spec, _scan_spec(b, SMALL_W, col_major, order),
                  _const_spec((1, DN_HEADS)), _const_spec((1, DN_HEADS)), state_spec],
        out_specs=[_scan_spec(b, DN_WIDTH, col_major, order), state_spec],
        out_shape=[jax.ShapeDtypeStruct(_by_column(o_shape, col_major), BF16),
                   jax.ShapeDtypeStruct(s0.shape, F32)],
        compiler_params=_params(("arbitrary",)),
        name=("dn_scan_rev" if reverse else "dn_scan_fwd") + ("_lat" if col_major else "_ctx"),
    )(q, k, v, p_small.reshape(_by_column(p_small.shape, col_major)), neg_a, dt_bias, s0)
    return o.reshape(o_shape), s


def _shifted(cur, prev, nxt, n, n_chunks, lat):
    rows = lax.broadcasted_iota(jnp.int32, cur.shape, 0)
    lane = lax.broadcasted_iota(jnp.int32, cur.shape, 1)
    has_prev = jnp.where(n > 0, 1.0, 0.0)
    has_next = jnp.where(n < n_chunks - 1, 1.0, 0.0)
    back = pltpu.roll(cur, 1, axis=0)
    fwd = pltpu.roll(cur, CHUNK - 1, axis=0)
    if lat:
        left = jnp.where(rows == 0, 0.0, back)
        right = jnp.where(rows == CHUNK - 1, 0.0, fwd)
        sel = lane % 4
        return jnp.where(sel == 0, left, jnp.where(sel == 1, right,
                         jnp.where(sel == 2, prev * has_prev, nxt * has_next)))
    before = jnp.where(rows == 0, prev[CHUNK - 1:CHUNK, :] * has_prev, back)
    after = jnp.where(rows == CHUNK - 1, nxt[0:1, :] * has_next, fwd)
    return jnp.where(lane % 2 == 0, before, after)


def _rw_prep_kernel(big_ref, bigp_ref, bign_ref, sm_ref, smp_ref, smn_ref, mub_ref, mus_ref,
                    wup_ref, aup_ref, gup_ref, w0_ref, a0_ref, kk_ref, ka_ref, rk_ref, seg_ref,
                    r_ref, v_ref, a_ref, gate_ref, bonus_ref, kd0_ref, kd1_ref, b0_ref, b1_ref,
                    lw0_ref, lw1_ref, *, n_chunks, lat):
    n = pl.program_id(1)
    big = big_ref[...].astype(F32)
    big_shift = _shifted(big, bigp_ref[...].astype(F32), bign_ref[...].astype(F32), n, n_chunks, lat)
    big = big + mub_ref[...] * (big_shift - big)
    sm = sm_ref[...]
    sm = sm + mus_ref[...] * (_shifted(sm, smp_ref[...], smn_ref[...], n, n_chunks, lat) - sm)
    r = big[:, :RW_WIDTH]
    k = big[:, RW_WIDTH:2 * RW_WIDTH]
    v = big[:, 2 * RW_WIDTH:]
    r_ref[...] = r.astype(BF16)
    v_ref[...] = v.astype(BF16)
    gate_ref[...] = _mm(_sigmoid(sm[:, GD_OFF:GD_OFF + G_LORA]), gup_ref[...]).astype(BF16)
    kx = k * kk_ref[...]
    kk = kx * lax.rsqrt(_seg_sum(kx * kx, seg_ref) + NORM_EPS)
    a_ref[...] = (-kk).astype(BF16)
    bonus = None
    for d, (kd_ref, b_ref, lw_ref) in enumerate(((kd0_ref, b0_ref, lw0_ref), (kd1_ref, b1_ref, lw1_ref))):
        wd = jnp.tanh(sm[:, d * W_LORA:(d + 1) * W_LORA])
        ad = sm[:, 2 * W_LORA + d * A_LORA:2 * W_LORA + (d + 1) * A_LORA]
        log_w = -_softplus(-(w0_ref[d:d + 1, :] + _mm(wd, wup_ref[d]))) - 0.5
        lw_ref[...] = -jnp.exp(log_w)
        iclr = _sigmoid(a0_ref[d:d + 1, :] + _mm(ad, aup_ref[d]))
        k_d = k * (1.0 + (iclr - 1.0) * ka_ref[...])
        kd_ref[...] = k_d.astype(BF16)
        b_ref[...] = (kk * iclr).astype(BF16)
        term = _seg_sum(r * k_d * rk_ref[...], seg_ref)
        bonus = term if bonus is None else bonus + term
    bonus_ref[...] = (bonus * v).astype(BF16)


def _rw_prep(p_rw, p_small, mu_big, mu_small, w_up, a_up, g_up, w0, a0, k_k, k_a, r_k, seg, *, lat):
    b, n_chunks = p_rw.shape[:2]
    clamp_prev = lambda n: jnp.maximum(n - 1, 0)
    clamp_next = lambda n: jnp.minimum(n + 1, n_chunks - 1)
    big = lambda f: _chunk_spec(QKV_W, False, f)
    small = lambda f: _chunk_spec(SMALL_W, False, f)
    same = lambda n: n
    out_spec = _chunk_spec(RW_WIDTH, False, same)
    out_sds = lambda dt: jax.ShapeDtypeStruct((b, n_chunks, CHUNK, RW_WIDTH), dt)
    row = lambda w: _const_spec((1, w))
    return pl.pallas_call(
        functools.partial(_rw_prep_kernel, n_chunks=n_chunks, lat=lat),
        grid=(b, n_chunks),
        in_specs=[big(same), big(clamp_prev), big(clamp_next),
                  small(same), small(clamp_prev), small(clamp_next),
                  row(QKV_W), row(SMALL_W),
                  _const_spec((2, W_LORA, RW_WIDTH)), _const_spec((2, A_LORA, RW_WIDTH)),
                  _const_spec((G_LORA, RW_WIDTH)), _const_spec((2, RW_WIDTH)), _const_spec((2, RW_WIDTH)),
                  row(RW_WIDTH), row(RW_WIDTH), row(RW_WIDTH), _const_spec((SEG_TILE, SEG_TILE))],
        out_specs=[out_spec] * 11,
        out_shape=[out_sds(BF16)] * 9 + [out_sds(F32)] * 2,
        compiler_params=_params(("parallel", "arbitrary")),
        name="rw_prep_lat" if lat else "rw_prep_ctx",
    )(p_rw, p_rw, p_rw, p_small, p_small, p_small, mu_big, mu_small, w_up, a_up, g_up, w0, a0, k_k, k_a, r_k, seg)


def _rw_kernel(r_ref, lw_ref, k_ref, v_ref, a_ref, b_ref, s0_ref, y_ref, s_ref, *, reverse):
    @pl.when(pl.program_id(0) == 0)
    def _():
        s_ref[...] = s0_ref[...]

    strict, _, same_block = _chunk_masks(reverse)
    _, incl2, _ = _chunk_masks(reverse, 2 * CHUNK)
    tri, _ = _cumsum_mats(reverse)
    n_batch = r_ref.shape[0]
    a_t, r_t, b_t, k_t, b_e, k_e, v_all, w_tot = ([] for _ in range(8))
    for b in range(n_batch):
        lw = lw_ref[b]
        cum = _mm_exact_lhs(tri, lw)
        tot = jnp.sum(lw, axis=0, keepdims=True)
        w_inv = jnp.exp(-cum)
        w_rest = jnp.exp(tot - cum)
        w_tot.append(jnp.exp(tot))
        a_t.append(a_ref[b].astype(F32) * jnp.exp(cum - lw))
        r_t.append(r_ref[b].astype(F32) * jnp.exp(cum))
        b_all = b_ref[b].astype(F32)
        k_all = k_ref[b].astype(F32)
        v_all.append(v_ref[b].astype(F32))
        b_t.append(b_all * w_inv)
        k_t.append(k_all * w_inv)
        b_e.append(b_all * w_rest)
        k_e.append(k_all * w_rest)

    ents = [(b, h) for b in range(n_batch) for h in range(RW_HEADS)]
    idx = range(len(ents))
    sls = [slice(h * RW_HEAD_DIM, (h + 1) * RW_HEAD_DIM) for _, h in ents]
    ahs = [a_t[b][:, sls[i]] for i, (b, h) in enumerate(ents)]
    rhs_ = [r_t[b][:, sls[i]] for i, (b, h) in enumerate(ents)]
    vhs = [v_all[b][:, sls[i]] for i, (b, h) in enumerate(ents)]
    gs = [_mm_nt(jnp.concatenate([ahs[i], rhs_[i]], axis=0),
                 jnp.concatenate([b_t[b][:, sls[i]], k_t[b][:, sls[i]]], axis=0))
          for i, (b, h) in enumerate(ents)]
    neg_ab = [jnp.where(strict, -g[:CHUNK, :CHUNK], 0.0) for g in gs]
    a_ak = [jnp.where(strict, g[:CHUNK, CHUNK:], 0.0) for g in gs]
    m_r = [jnp.where(incl2, g[CHUNK:, :], 0.0) for g in gs]
    t_corr = _unit_tri_inverse_minus_eye(neg_ab, same_block)
    rhs = [jnp.concatenate([ahs[i], _mm(a_ak[i], vhs[i])], axis=1) for i in idx]
    sols = [rhs[i] + _mm(t_corr[i], rhs[i]) for i in idx]
    states = [s_ref[b, h] for b, h in ents]
    lhs = [_mm_nt(jnp.concatenate([sols[i][:, :RW_HEAD_DIM], rhs_[i]], axis=0), states[i])
           for i in idx]
    uvs = [jnp.concatenate([lhs[i][:CHUNK] + sols[i][:, RW_HEAD_DIM:], vhs[i]], axis=0) for i in idx]
    ys = [lhs[i][CHUNK:] + _mm(m_r[i], uvs[i]) for i in idx]
    for i in range(0, len(ents), 2):
        b, h = ents[i]
        y_ref[b, :, h * RW_HEAD_DIM:(h + 2) * RW_HEAD_DIM] = jnp.concatenate(
            [ys[i], ys[i + 1]], axis=1).astype(y_ref.dtype)
    for i, (b, h) in enumerate(ents):
        bk = jnp.concatenate([b_e[b][:, sls[i]], k_e[b][:, sls[i]]], axis=0)
        s_ref[b, h] = states[i] * w_tot[b][:, sls[i]] + _mm_tn(uvs[i], bk)


def _rw_scan(r, lw, k, v, a, bvec, s0, *, reverse, tag):
    b, n_chunks = r.shape[:2]
    order = (lambda n: n_chunks - 1 - n) if reverse else (lambda n: n)
    tok_spec = _scan_spec(b, RW_WIDTH, False, order)
    state_spec = _const_spec(s0.shape)
    return pl.pallas_call(
        functools.partial(_rw_kernel, reverse=reverse),
        grid=(n_chunks,),
        in_specs=[tok_spec] * 6 + [state_spec],
        out_specs=[tok_spec, state_spec],
        out_shape=[jax.ShapeDtypeStruct(r.shape, BF16), jax.ShapeDtypeStruct(s0.shape, F32)],
        compiler_params=_params(("arbitrary",)),
        name=("rw_scan_rev_" if reverse else "rw_scan_fwd_") + tag,
    )(r, lw, k, v, a, bvec, s0)


def _mix_post_kernel(of_ref, or_ref, z_ref, yf_ref, yr_ref, bonus_ref, gate_ref, og_ref, lnw_ref,
                     lnb_ref, seg_ref, o_ref):
    for h in range(DN_HEADS):
        sl = slice(h * DN_HEAD_DIM, (h + 1) * DN_HEAD_DIM)
        o = of_ref[:, sl].astype(F32) + or_ref[:, sl].astype(F32)
        o = o * lax.rsqrt(jnp.mean(o * o, axis=-1, keepdims=True) + NORM_EPS) * og_ref[...]
        z = z_ref[:, sl].astype(F32)
        o_ref[:, sl] = (o * (z * _sigmoid(z))).astype(o_ref.dtype)
    y = yf_ref[...].astype(F32) + yr_ref[...].astype(F32)
    dev = y - _seg_sum(y, seg_ref) * (1.0 / RW_HEAD_DIM)
    var = _seg_sum(dev * dev, seg_ref) * (1.0 / RW_HEAD_DIM)
    yn = dev * lax.rsqrt(var + RW_LN_EPS)
    out = (yn * lnw_ref[...] + lnb_ref[...] + bonus_ref[...].astype(F32)) * gate_ref[...].astype(F32)
    o_ref[:, DN_WIDTH:] = out.astype(o_ref.dtype)


def _mix_post(o_f, o_r, z, y_f, y_r, bonus, gate, out_g, ln_w, ln_b, seg):
    b, n_chunks = o_f.shape[:2]
    same = lambda n: n
    tok = _chunk_spec(DN_WIDTH, False, same)
    row = lambda w: _const_spec((1, w))
    return pl.pallas_call(
        _mix_post_kernel,
        grid=(b, n_chunks),
        in_specs=[tok] * 7 + [row(DN_HEAD_DIM), row(RW_WIDTH), row(RW_WIDTH),
                              _const_spec((SEG_TILE, SEG_TILE))],
        out_specs=_chunk_spec(D_MODEL, False, same),
        out_shape=jax.ShapeDtypeStruct((b, n_chunks, CHUNK, D_MODEL), BF16),
        compiler_params=_params(("parallel", "arbitrary")),
        name="mix_post",
    )(o_f, o_r, z, y_f, y_r, bonus, gate, out_g, ln_w, ln_b, seg)


def _split_w_in(w):
    dn_qkv, dn_z, dn_gates = w[:, :QKV_W], w[:, QKV_W:4 * DN_WIDTH], w[:, 4 * DN_WIDTH:P_DN]
    rw_rkv, rw_small = w[:, P_DN:P_DN + QKV_W], w[:, P_DN + QKV_W:]
    pad = jnp.zeros((w.shape[0], SMALL_W - DN_GATE_OFF - 4 * DN_HEADS), w.dtype)
    return rw_rkv, dn_qkv, dn_z, jnp.concatenate([rw_small, dn_gates, pad], axis=1)


def kernel(x, c, ctx, c_ctx, w_mod, b_mod, mix_pre_g, mix_post_g, ffn_pre_g, ffn_post_g, w_in,
           dn_conv, dn_a_log, dn_dt_bias, dn_out_g, rw_mu, rw_w0, rw_w_up, rw_a0, rw_a_up, rw_g_up,
           rw_k_k, rw_k_a, rw_r_k, rw_ln_w, rw_ln_b, w_out, w_ffn_in, w_ffn_out):
    bsz, seq, d = x.shape
    n_ctx = ctx.shape[1]
    rows = seq // GRID_W
    assert GRID_W == CHUNK and rows == CHUNK and n_ctx % CHUNK == 0

    cond = jnp.concatenate([jax.nn.silu(c), jax.nn.silu(c_ctx)[None]], axis=0)
    cond = jnp.pad(cond, ((0, 8 - cond.shape[0]), (0, 0)))

    head_ids = jnp.arange(SEG_TILE) // RW_HEAD_DIM
    seg = (head_ids[:, None] == head_ids[None, :]).astype(BF16)

    streams = [
        dict(h=ctx.reshape(bsz, n_ctx // CHUNK, CHUNK, d), lat=False, tm=256, tm_pro=256),
        dict(h=x.reshape(bsz, rows, GRID_W, d), lat=True, tm=512, tm_pro=1024),
    ]

    for l in range(DEPTH):
        last = l == DEPTH - 1
        mod_l = _matmul(cond, w_mod[l], tm=8, tn=1536, name="adaln") + b_mod[l]
        mod_l = mod_l.reshape(8, 6, d)
        mods = [mod_l[bsz:bsz + 1, :, None, :], mod_l[:bsz, :, None, :]]

        w_in_groups = [w.astype(BF16) for w in _split_w_in(w_in[l])]
        conv_w = jnp.pad(dn_conv[l], ((0, 8 - CONV_K), (0, 0)))
        mu = rw_mu[l]
        mu_big = mu[None, :QKV_W]
        mu_small = jnp.pad(mu[QKV_W:], (0, SMALL_W - (P_RW - QKV_W)))[None]
        w_up = rw_w_up[l].astype(BF16)
        a_up = rw_a_up[l].astype(BF16)
        g_up = rw_g_up[l].astype(BF16)
        neg_a = -jnp.exp(dn_a_log[l])
        dt_b = dn_dt_bias[l]
        out_g = dn_out_g[l][None]
        w_out_l = w_out[l].astype(BF16)
        w_i = w_ffn_in[l].astype(BF16)
        w_o = w_ffn_out[l].astype(BF16)

        dn_state = [jnp.zeros((bsz, DN_HEADS, DN_HEAD_DIM, DN_HEAD_DIM), F32) for _ in range(2)]
        rw_state = [jnp.zeros((bsz, RW_HEADS, RW_HEAD_DIM, RW_HEAD_DIM), F32) for _ in range(2)]

        for si, st in enumerate(streams):
            h = st["h"]
            lat, tm, tm_pro = st["lat"], st["tm"], st["tm_pro"]
            nb = bsz if lat else 1
            n_chunks = h.shape[1]
            m = bsz * n_chunks * CHUNK
            mod = mods[si]
            tag = "lat" if lat else "ctx"

            u = _norm_mod_rows(h.reshape(m, d), mix_pre_g[l][None], mod[:, 0], mod[:, 1], tm=tm,
                               name="mix_pre_" + tag)
            p_rw, p_dn, p_z, p_small = [
                _matmul(u, w, tm=tm_pro, tn=min(w.shape[1], 1024),
                        out_dtype=F32 if w.shape[1] == SMALL_W else BF16,
                        name="in_proj_" + tag).reshape(bsz, n_chunks, CHUNK, w.shape[1])
                for w in w_in_groups]

            q, k, v = _dn_prep(p_dn, conv_w, col_major=lat)
            dn_o = []
            for dr in range(2):
                o, dn_state[dr] = _dn_scan(q, k, v, p_small, neg_a[dr][None], dt_b[dr][None], dn_state[dr],
                                           col_major=lat, reverse=(dr == 1))
                dn_o.append(o)

            (r, vr, a, gate, bonus, kd0, kd1, b0, b1, lw0, lw1) = _rw_prep(
                p_rw, p_small, mu_big, mu_small, w_up, a_up, g_up, rw_w0[l], rw_a0[l], rw_k_k[l][None],
                rw_k_a[l][None], rw_r_k[l].reshape(1, RW_WIDTH), seg, lat=lat)
            rw_y = []
            for dr, (kd, bv, lw) in enumerate(((kd0, b0, lw0), (kd1, b1, lw1))):
                y, rw_state[dr] = _rw_scan(r, lw, kd, vr, a, bv, rw_state[dr], reverse=(dr == 1), tag=tag)
                rw_y.append(y)

            if last and not lat:
                continue

            mix = _mix_post(dn_o[0], dn_o[1], p_z, rw_y[0], rw_y[1], bonus, gate, out_g,
                            rw_ln_w[l][None], rw_ln_b[l][None], seg)
            h2 = _matmul_residual(mix.reshape(m, d), w_out_l, h.reshape(m, d), mix_post_g[l][None],
                                  mod[:, 2], tm=tm, tk=d, name="out_proj_" + tag)

            hid = _norm_mod_matmul(h2, ffn_pre_g[l][None], mod[:, 3], mod[:, 4], w_i, swiglu=True,
                                   tm=tm_pro, tn=512, out_dtype=BF16, name="ffn_in_" + tag)
            h3 = _matmul_residual(hid, w_o, h2, ffn_post_g[l][None], mod[:, 5], tm=tm,
                                  tk=FFN_HIDDEN // 4, name="ffn_out_" + tag)
            st["h"] = h3.reshape(bsz, n_chunks, CHUNK, d)

    return streams[1]["h"].reshape(bsz, seq, d)
```
